```python
import jax
import jax.numpy as jnp
from jax import lax
import numpy as np

D_MODEL = 1024
BATCH = 32
SEQ = 256
DEPTH = 4
DEC_BATCH = 4
DEC_SEQ = 4096
PAST_LEN = 512

GRID_W = 64
N_EVEN = (DEPTH + 1) // 2
N_ODD = DEPTH // 2
EPS = 1e-6
NEG_INF = -1e30
ROPE_BASE = 10000.0
ATTN_Q_BLOCK = 128

A_HEADS = 4
A_DK = 64
A_DV = 128
A_GATE_RANK = 16
A_GATE_TAU = 16.0
A_CHUNK = 64

B_HEADS = 8
B_Q_RANK = 256
B_KV_RANK = 256
B_NOPE = 64
B_ROPE = 64
B_V = 64

C_GROUPS = 4
C_GROUP_W = 128
C_WINDOWS = (2, 4, 8, 16)

D_HEADS = 8
D_KV_HEADS = 2
D_HEAD = 64
D_WINDOW = 128
D_BLOCK = 128

P_HEADS = 8
P_NKEYS = 128
P_EXPERTS = P_NKEYS * P_NKEYS
P_QDIM = 256
P_TOPK = 16
P_TOKEN_BLOCK = 128

EVEN_SPLIT = (A_HEADS * A_DK, A_HEADS * A_DK, A_HEADS * A_DV, A_HEADS * A_DV, A_GATE_RANK, A_GATE_RANK, B_Q_RANK, B_KV_RANK, B_ROPE)
EVEN_IN = sum(EVEN_SPLIT)
EVEN_MIX = A_HEADS * A_DV + B_HEADS * B_V
ODD_SPLIT = (C_GROUPS * C_GROUP_W, D_HEADS * D_HEAD, D_KV_HEADS * D_HEAD, D_KV_HEADS * D_HEAD)
ODD_IN = sum(ODD_SPLIT)
ODD_MIX = C_GROUPS * C_GROUP_W + D_HEADS * D_HEAD

kernel_name = 'hybrid_diffusion_prefix_trunk_step'


def split_cols(x, sizes):
    return jnp.split(x, np.cumsum(sizes)[:-1].tolist(), axis=-1)


def rms_norm(x, g):
    xf = x.astype(jnp.float32)
    y = xf * lax.rsqrt(jnp.mean(xf * xf, axis=-1, keepdims=True) + EPS)
    return (y * g.astype(jnp.float32)).astype(x.dtype)


def ada_mod(cond, w, b):
    m = jax.nn.silu(cond) @ w + b
    return jnp.split(m[..., None, :], 6, axis=-1)


def modulate(x, g, shift, scale):
    return rms_norm(x, g) * (1.0 + scale) + shift


def axial_rope(rows, rot_dim):
    quarter = rot_dim // 4
    inv = ROPE_BASE ** (-jnp.arange(quarter, dtype=jnp.float32) / quarter)
    row = jnp.repeat(jnp.arange(rows, dtype=jnp.float32), GRID_W)
    col = jnp.tile(jnp.arange(GRID_W, dtype=jnp.float32), rows)
    ar = row[:, None] * inv
    ac = col[:, None] * inv
    ang = jnp.concatenate([ar, ar, ac, ac], axis=-1)
    return jnp.cos(ang), jnp.sin(ang)


def rope_last(x, cos, sin):
    r = cos.shape[-1]
    xa, xr = x[..., :-r], x[..., -r:]
    q4 = r // 4
    x4 = xr.reshape(xr.shape[:-1] + (2, 2, q4))
    rot = jnp.stack([-x4[..., 1, :], x4[..., 0, :]], axis=-2).reshape(xr.shape)
    c = cos[None, :, None, :].astype(x.dtype)
    s = sin[None, :, None, :].astype(x.dtype)
    return jnp.concatenate([xa, xr * c + rot * s], axis=-1)


def gla_log_decay(low, w_up, b_up):
    bsz, t, _ = low.shape
    z = (low @ w_up + b_up).astype(jnp.float32)
    return (jax.nn.log_sigmoid(z) / A_GATE_TAU).reshape(bsz, t, A_HEADS, A_DK)


def gla_chunked(q, k, v, log_a, s0):
    bsz, t, h, dk = q.shape
    dv = v.shape[-1]
    n, l = t // A_CHUNK, A_CHUNK
    f32 = jnp.float32
    qc = q.reshape(bsz, n, l, h, dk).astype(f32)
    kc = k.reshape(bsz, n, l, h, dk).astype(f32)
    vc = v.reshape(bsz, n, l, h, dv).astype(f32)
    b = jnp.cumsum(log_a.reshape(bsz, n, l, h, dk).astype(f32), axis=2)
    b_last = b[:, :, -1]
    q_dec = qc * jnp.exp(b)
    k_intra = kc * jnp.exp(-b)
    k_state = kc * jnp.exp(b_last[:, :, None] - b)
    lower = jnp.tril(jnp.ones((l, l), dtype=bool))
    att = jnp.where(lower, jnp.einsum('bnihd,bnjhd->bnhij', q_dec, k_intra), 0.0)
    o_intra = jnp.einsum('bnhij,bnjhv->bnihv', att, vc)
    upd = jnp.einsum('bnjhd,bnjhv->bnhdv', k_state, vc)
    decay = jnp.exp(b_last)

    def step(s, inp):
        d, u = inp
        return d[..., None] * s + u, s

    s_fin, s_start = lax.scan(step, s0.astype(f32), (jnp.moveaxis(decay, 1, 0), jnp.moveaxis(upd, 1, 0)))
    o_inter = jnp.einsum('bnihd,bnhdv->bnihv', q_dec, jnp.moveaxis(s_start, 0, 1))
    o = (o_intra + o_inter).reshape(bsz, t, h, dv)
    return o.astype(v.dtype), s_fin.astype(s0.dtype)


def gla_bidir(q, k, v, la_f, la_b, s0_f, s0_b):
    o_f, s_f = gla_chunked(q, k, v, la_f, s0_f)
    flip = lambda a: jnp.flip(a, axis=1)
    o_b, s_b = gla_chunked(flip(q), flip(k), flip(v), flip(la_b), s0_b)
    return o_f + flip(o_b), s_f, s_b


def attn_blocked(q, k, v, sink):
    bsz, t, hk, g, dq = q.shape
    nb = t // ATTN_Q_BLOCK
    scale = dq ** -0.5
    s_len = k.shape[1]
    qb = jnp.moveaxis(q.reshape(bsz, nb, ATTN_Q_BLOCK, hk, g, dq), 1, 0)

    def one(qi):
        s = jnp.einsum('bqkgd,bskd->bkgqs', qi, k).astype(jnp.float32) * scale
        if sink is not None:
            sk = jnp.broadcast_to(sink.astype(jnp.float32)[None, :, :, None, None], s.shape[:-1] + (1,))
            s = jnp.concatenate([s, sk], axis=-1)
        p = jax.nn.softmax(s, axis=-1)[..., :s_len]
        return jnp.einsum('bkgqs,bskd->bqkgd', p.astype(v.dtype), v)

    o = lax.map(one, qb)
    return jnp.moveaxis(o, 0, 1).reshape(bsz, t, hk * g * v.shape[-1])


def window_attn(q, k, v, k_ctx, v_ctx, sink):
    bsz, t, hk, g, d = q.shape
    nb = t // D_BLOCK
    span = D_BLOCK + 2 * D_WINDOW
    p_len = k_ctx.shape[1]
    padw = ((0, 0), (D_WINDOW, D_WINDOW), (0, 0), (0, 0))
    kp = jnp.pad(k, padw)
    vp = jnp.pad(v, padw)
    scale = d ** -0.5
    sk = sink.astype(jnp.float32)[None, :, :, None, None]

    def one(bi):
        start = bi * D_BLOCK
        qi = lax.dynamic_slice_in_dim(q, start, D_BLOCK, axis=1)
        ki = lax.dynamic_slice_in_dim(kp, start, span, axis=1)
        vi = lax.dynamic_slice_in_dim(vp, start, span, axis=1)
        qpos = start + jnp.arange(D_BLOCK)
        kpos = start - D_WINDOW + jnp.arange(span)
        valid = (jnp.abs(qpos[:, None] - kpos[None, :]) <= D_WINDOW) & (kpos >= 0)[None, :] & (kpos < t)[None, :]
        s_loc = jnp.einsum('bqkgd,bskd->bkgqs', qi, ki).astype(jnp.float32) * scale
        s_loc = jnp.where(valid, s_loc, NEG_INF)
        s_ctx = jnp.einsum('bqkgd,bpkd->bkgqp', qi, k_ctx).astype(jnp.float32) * scale
        s = jnp.concatenate([s_loc, s_ctx, jnp.broadcast_to(sk, s_loc.shape[:-1] + (1,))], axis=-1)
        p = jax.nn.softmax(s, axis=-1)
        o = jnp.einsum('bkgqs,bskd->bqkgd', p[..., :span].astype(v.dtype), vi)
        return o + jnp.einsum('bkgqp,bpkd->bqkgd', p[..., span:span + p_len].astype(v.dtype), v_ctx)

    o = lax.map(one, jnp.arange(nb))
    return jnp.moveaxis(o, 0, 1).reshape(bsz, t, hk * g * d)


def mla_keys(c_kv, k_rope, w_uk, w_uv, g_kn):
    bsz, s, _ = c_kv.shape
    k_nope = (c_kv @ w_uk).reshape(bsz, s, B_HEADS, B_NOPE)
    k_r = jnp.broadcast_to(k_rope[:, :, None, :], (bsz, s, B_HEADS, B_ROPE)).astype(k_nope.dtype)
    k = rms_norm(jnp.concatenate([k_nope, k_r], axis=-1), g_kn)
    v = (c_kv @ w_uv).reshape(bsz, s, B_HEADS, B_V)
    return k, v


def pool_mixer(xc, w_pool, scale):
    bsz, t, _ = xc.shape
    xg = xc.reshape(bsz, t, C_GROUPS, C_GROUP_W).astype(jnp.float32)
    cs = jnp.concatenate([jnp.zeros_like(xg[:, :1]), jnp.cumsum(xg, axis=1)], axis=1)
    pos = jnp.arange(t)
    pooled = []
    for gi, w in enumerate(C_WINDOWS):
        lo = jnp.clip(pos - w // 2, 0, t)
        hi = jnp.clip(pos - w // 2 + w, 0, t)
        csg = cs[:, :, gi]
        pooled.append((csg[:, hi] - csg[:, lo]) / (hi - lo).astype(jnp.float32)[None, :, None])
    y = jnp.stack(pooled, axis=2) - xg
    y = jnp.einsum('btgc,gcd->btgd', y, w_pool.astype(jnp.float32))
    return (y.reshape(bsz, t, C_GROUPS * C_GROUP_W) * scale).astype(xc.dtype)


def peer_ffn(h, wq, subkeys, u_tab, v_tab):
    bsz, t, d = h.shape
    x = h.reshape(-1, P_TOKEN_BLOCK, d)

    def one(xb):
        nt = xb.shape[0]
        q = (xb @ wq).reshape(nt, P_HEADS, 2, P_QDIM // 2)
        s = jnp.einsum('thcd,hckd->thck', q, subkeys).astype(jnp.float32)
        s1, i1 = lax.top_k(s[:, :, 0], P_TOPK)
        s2, i2 = lax.top_k(s[:, :, 1], P_TOPK)
        cand_s = (s1[..., :, None] + s2[..., None, :]).reshape(nt, P_HEADS, P_TOPK * P_TOPK)
        cand_i = (i1[..., :, None] * P_NKEYS + i2[..., None, :]).reshape(nt, P_HEADS, P_TOPK * P_TOPK)
        top_s, pos = lax.top_k(cand_s, P_TOPK)
        idx = jnp.take_along_axis(cand_i, pos, axis=-1)
        gate = jax.nn.softmax(top_s, axis=-1).astype(xb.dtype)
        u = jnp.take(u_tab, idx, axis=0)
        act = jax.nn.gelu(jnp.einsum('thkd,td->thk', u, xb), approximate=False) * gate
        vv = jnp.take(v_tab, idx, axis=0)
        return jnp.einsum('thk,thkd->td', act, vv)

    return lax.map(one, x).reshape(bsz, t, d)


def even_mixer(h, p, ctx, rope):
    (w_in, w_gu, b_gu, g_go, g_cq, g_ckv, w_uq, w_uk, w_uv, g_qn, g_kn, w_out) = p
    bsz, t, _ = h.shape
    qa, ka, va, ra, glf, glb, cq, ckv_raw, kr = split_cols(h @ w_in, EVEN_SPLIT)
    qa = qa.reshape(bsz, t, A_HEADS, A_DK) * (A_DK ** -0.5)
    ka = ka.reshape(bsz, t, A_HEADS, A_DK)
    va = va.reshape(bsz, t, A_HEADS, A_DV)
    la_f = gla_log_decay(glf, w_gu[0], b_gu[0])
    la_b = gla_log_decay(glb, w_gu[1], b_gu[1])
    if ctx is None:
        s0_f = jnp.zeros((bsz, A_HEADS, A_DK, A_DV), h.dtype)
        s0_b = s0_f
    else:
        s0_f, s0_b, ckv_ctx, kr_ctx = ctx
    oa, s_f, s_b = gla_bidir(qa, ka, va, la_f, la_b, s0_f, s0_b)
    oa = rms_norm(oa, g_go) * jax.nn.silu(ra).reshape(bsz, t, A_HEADS, A_DV)
    c_kv = rms_norm(ckv_raw, g_ckv)
    qb = (rms_norm(cq, g_cq) @ w_uq).reshape(bsz, t, B_HEADS, B_NOPE + B_ROPE)
    qb = rms_norm(qb, g_qn)
    kb, vb = mla_keys(c_kv, kr, w_uk, w_uv, g_kn)
    if ctx is None:
        new_ctx = (s_f, s_b, c_kv, kr)
    else:
        cos, sin = rope
        qb = rope_last(qb, cos, sin)
        kb = rope_last(kb, cos, sin)
        kc, vc = mla_keys(ckv_ctx, kr_ctx, w_uk, w_uv, g_kn)
        kb = jnp.concatenate([kb, kc], axis=1)
        vb = jnp.concatenate([vb, vc], axis=1)
        new_ctx = None
    ob = attn_blocked(qb[:, :, :, None, :], kb, vb, None)
    out = jnp.concatenate([oa.reshape(bsz, t, A_HEADS * A_DV), ob], axis=-1) @ w_out
    return out, new_ctx


def odd_mixer(h, p, ctx, rope):
    (w_in, w_pool, p_scale, g_qn, g_kn, sink, w_out) = p
    bsz, t, _ = h.shape
    g = D_HEADS // D_KV_HEADS
    xc, qd, kd, vd = split_cols(h @ w_in, ODD_SPLIT)
    oc = pool_mixer(xc, w_pool, p_scale)
    q = rms_norm(qd.reshape(bsz, t, D_HEADS, D_HEAD), g_qn)
    k = rms_norm(kd.reshape(bsz, t, D_KV_HEADS, D_HEAD), g_kn)
    v = vd.reshape(bsz, t, D_KV_HEADS, D_HEAD)
    sink_g = sink.reshape(D_KV_HEADS, g)
    if ctx is None:
        od = attn_blocked(q.reshape(bsz, t, D_KV_HEADS, g, D_HEAD), k, v, sink_g)
        new_ctx = (k, v)
    else:
        k_ctx, v_ctx = ctx
        cos, sin = rope
        q = rope_last(q, cos, sin)
        k = rope_last(k, cos, sin)
        od = window_attn(q.reshape(bsz, t, D_KV_HEADS, g, D_HEAD), k, v, k_ctx, v_ctx, sink_g)
        new_ctx = None
    out = jnp.concatenate([oc, od], axis=-1) @ w_out
    return out, new_ctx


def setup_inputs(seed: int = 0) -> dict:
    key = jax.random.key(seed)
    ks = iter(jax.random.split(key, 40))
    nrm = lambda shape, s: jax.random.normal(next(ks), shape, jnp.float32) * s
    gain = lambda shape: 1.0 + 0.05 * jax.random.normal(next(ks), shape, jnp.float32)
    d = D_MODEL
    sd = d ** -0.5
    return {
        'x_prompt': nrm((BATCH, SEQ, d), 1.0),
        'x_sample': nrm((DEC_BATCH, DEC_SEQ, d), 1.0),
        'state_gla': nrm((DEC_BATCH, N_EVEN, 2, A_HEADS, A_DK, A_DV), 1.0),
        'cache_mla_ckv': nrm((DEC_BATCH, N_EVEN, PAST_LEN, B_KV_RANK), 1.0),
        'cache_mla_krope': nrm((DEC_BATCH, N_EVEN, PAST_LEN, B_ROPE), 1.0),
        'cache_win_kv': nrm((DEC_BATCH, N_ODD, 2, PAST_LEN, D_KV_HEADS, D_HEAD), 1.0),
        'c': nrm((DEC_BATCH, d), 1.0),
        'c_ctx': nrm((d,), 1.0),
        'g_norm': gain((DEPTH, 2, d)),
        'w_ada': nrm((DEPTH, d, 6 * d), 0.5 * sd),
        'b_ada': nrm((DEPTH, 6 * d), 0.02),
        'w_in_even': nrm((N_EVEN, d, EVEN_IN), sd),
        'w_gate_up': nrm((N_EVEN, 2, A_GATE_RANK, A_HEADS * A_DK), A_GATE_RANK ** -0.5),
        'b_gate_up': nrm((N_EVEN, 2, A_HEADS * A_DK), 0.1),
        'g_gla_out': gain((N_EVEN, A_DV)),
        'g_mla_cq': gain((N_EVEN, B_Q_RANK)),
        'g_mla_ckv': gain((N_EVEN, B_KV_RANK)),
        'w_mla_uq': nrm((N_EVEN, B_Q_RANK, B_HEADS * (B_NOPE + B_ROPE)), B_Q_RANK ** -0.5),
        'w_mla_uk': nrm((N_EVEN, B_KV_RANK, B_HEADS * B_NOPE), B_KV_RANK ** -0.5),
        'w_mla_uv': nrm((N_EVEN, B_KV_RANK, B_HEADS * B_V), B_KV_RANK ** -0.5),
        'g_mla_qn': gain((N_EVEN, B_NOPE + B_ROPE)),
        'g_mla_kn': gain((N_EVEN, B_NOPE + B_ROPE)),
        'w_out_even': nrm((N_EVEN, EVEN_MIX, d), EVEN_MIX ** -0.5),
        'w_in_odd': nrm((N_ODD, d, ODD_IN), sd),
        'w_pool': nrm((N_ODD, C_GROUPS, C_GROUP_W, C_GROUP_W), C_GROUP_W ** -0.5),
        'pool_scale': gain((N_ODD, C_GROUPS * C_GROUP_W)),
        'g_win_qn': gain((N_ODD, D_HEAD)),
        'g_win_kn': gain((N_ODD, D_HEAD)),
        'win_sink': nrm((N_ODD, D_HEADS), 0.5),
        'w_out_odd': nrm((N_ODD, ODD_MIX, d), ODD_MIX ** -0.5),
        'peer_wq': nrm((DEPTH, d, P_HEADS * P_QDIM), sd),
        'peer_subkeys': nrm((DEPTH, P_HEADS, 2, P_NKEYS, P_QDIM // 2), (P_QDIM // 2) ** -0.5),
        'peer_u': nrm((DEPTH, P_EXPERTS, d), sd),
        'peer_v': nrm((DEPTH, P_EXPERTS, d), (P_HEADS * P_TOPK) ** -0.5),
    }


def reference(x_prompt, x_sample, state_gla, cache_mla_ckv, cache_mla_krope, cache_win_kv, c, c_ctx,
              g_norm, w_ada, b_ada, w_in_even, w_gate_up, b_gate_up, g_gla_out, g_mla_cq, g_mla_ckv,
              w_mla_uq, w_mla_uk, w_mla_uv, g_mla_qn, g_mla_kn, w_out_even, w_in_odd, w_pool, pool_scale,
              g_win_qn, g_win_kn, win_sink, w_out_odd, peer_wq, peer_subkeys, peer_u, peer_v):
    rows = x_sample.shape[1] // GRID_W
    rope_b = axial_rope(rows, B_ROPE)
    rope_d = axial_rope(rows, D_HEAD)
    xp, xs = x_prompt, x_sample
    gla_states, mla_ckv, mla_kr, win_kv = [], [], [], []
    for l in range(DEPTH):
        sh1p, sc1p, gt1p, sh2p, sc2p, gt2p = ada_mod(c_ctx, w_ada[l], b_ada[l])
        sh1s, sc1s, gt1s, sh2s, sc2s, gt2s = ada_mod(c, w_ada[l], b_ada[l])
        hp = modulate(xp, g_norm[l, 0], sh1p, sc1p)
        hs = modulate(xs, g_norm[l, 0], sh1s, sc1s)
        if l % 2 == 0:
            e = l // 2
            pe = (w_in_even[e], w_gate_up[e], b_gate_up[e], g_gla_out[e], g_mla_cq[e], g_mla_ckv[e],
                  w_mla_uq[e], w_mla_uk[e], w_mla_uv[e], g_mla_qn[e], g_mla_kn[e], w_out_even[e])
            mp, (s_f, s_b, ckv_p, kr_p) = even_mixer(hp, pe, None, None)
            ctx_s = (state_gla[:, e, 0], state_gla[:, e, 1], cache_mla_ckv[:, e], cache_mla_krope[:, e])
            ms, _ = even_mixer(hs, pe, ctx_s, rope_b)
            gla_states.append(jnp.stack([s_f, s_b], axis=1))
            mla_ckv.append(ckv_p)
            mla_kr.append(kr_p)
        else:
            o = l // 2
            po = (w_in_odd[o], w_pool[o], pool_scale[o], g_win_qn[o], g_win_kn[o], win_sink[o], w_out_odd[o])
            mp, (k_p, v_p) = odd_mixer(hp, po, None, None)
            ms, _ = odd_mixer(hs, po, (cache_win_kv[:, o, 0], cache_win_kv[:, o, 1]), rope_d)
            win_kv.append(jnp.stack([k_p, v_p], axis=1))
        xp = xp + gt1p * mp
        xs = xs + gt1s * ms
        hp = modulate(xp, g_norm[l, 1], sh2p, sc2p)
        hs = modulate(xs, g_norm[l, 1], sh2s, sc2s)
        xp = xp + gt2p * peer_ffn(hp, peer_wq[l], peer_subkeys[l], peer_u[l], peer_v[l])
        xs = xs + gt2s * peer_ffn(hs, peer_wq[l], peer_subkeys[l], peer_u[l], peer_v[l])
    state_gla_new = jnp.stack(gla_states, axis=1)
    cache_mla_ckv_new = jnp.stack(mla_ckv, axis=1)
    cache_mla_krope_new = jnp.stack(mla_kr, axis=1)
    cache_win_kv_new = jnp.stack(win_kv, axis=1)
    return (xp, xs, state_gla_new, cache_mla_ckv_new, cache_mla_krope_new, cache_win_kv_new)
```

```python
import functools

import jax
import jax.numpy as jnp
import numpy as np
from jax import lax
from jax.experimental import pallas as pl
from jax.experimental.pallas import tpu as pltpu

F32 = jnp.float32
BF16 = jnp.bfloat16
I32 = jnp.int32

D = 1024
N_PROMPT_SEQ = 32
PROMPT_LEN = 256
N_SAMPLE_SEQ = 4
SAMPLE_LEN = 4096
PAST = 512
T_P = N_PROMPT_SEQ * PROMPT_LEN
T_S = N_SAMPLE_SEQ * SAMPLE_LEN
T = T_P + T_S
GRID_W = 64
EPS = 1e-6
NEG_INF = -1e30
ROPE_BASE = 10000.0

TM = 256
NBLK = T // TM
NBLK_P = T_P // TM
SBLK = SAMPLE_LEN // TM

EVEN_W = 2176
ODD_W = 1280
POOL_WINDOWS = (2, 4, 8, 16)
POOL_HALO = 8

P_EXPERTS = 16384
NPICK = 128
PEER_TB = 64
PEER_TG = 8
VMEM_LIMIT = 56 * 1024 * 1024


def _cparams(sem, vmem=None):
    return pltpu.CompilerParams(dimension_semantics=sem, vmem_limit_bytes=vmem)


def _mod_group(i):
    return jnp.where(i < NBLK_P, 0, 1 + (i - NBLK_P) // SBLK)


def _split_bf16(x):
    hi = x.astype(BF16)
    lo = (x - hi.astype(F32)).astype(BF16)
    return hi, lo


def _dot(a, b):
    return jnp.dot(a, b, preferred_element_type=F32)


def _dot_nt(a, b):
    return lax.dot_general(a, b, (((1,), (1,)), ((), ())), preferred_element_type=F32)


def _rms(x, g):
    ms = jnp.mean(x * x, axis=-1, keepdims=True)
    return x * lax.rsqrt(ms + EPS) * g


def _ada_kernel(c_ref, w_ref, b_ref, o_ref):
    c = c_ref[...]
    s = c / (1.0 + jnp.exp(-c))
    o_ref[0] = _dot(s.astype(BF16), w_ref[0].astype(BF16)) + b_ref[0]


def ada_call(cond8, w_ada, b_ada):
    depth, _, n6 = w_ada.shape
    tn = 1536
    return pl.pallas_call(
        _ada_kernel,
        grid=(depth, n6 // tn),
        in_specs=[
            pl.BlockSpec((8, D), lambda l, j: (0, 0)),
            pl.BlockSpec((1, D, tn), lambda l, j: (l, 0, j)),
            pl.BlockSpec((1, 1, tn), lambda l, j: (l, 0, j)),
        ],
        out_specs=pl.BlockSpec((1, 8, tn), lambda l, j: (l, 0, j)),
        out_shape=jax.ShapeDtypeStruct((depth, 8, n6), F32),
        compiler_params=_cparams(("arbitrary", "arbitrary"), 40 * 1024 * 1024),
        name="ada",
    )(cond8, w_ada, b_ada.reshape(depth, 1, n6))


def _modproj_kernel(x_ref, g_ref, mod_ref, w_ref, o_ref, *h_ref, a):
    x = x_ref[...]
    m = mod_ref[0]
    h = _rms(x, g_ref[...]) * (1.0 + m[a + 1:a + 2]) + m[a:a + 1]
    o_ref[...] = _dot(h.astype(BF16), w_ref[...]).astype(o_ref.dtype)
    if h_ref:
        h_ref[0][...] = h


def modproj_call(x, g, mods, w, a, out_dtype, with_h):
    n = w.shape[1]
    out_shape = [jax.ShapeDtypeStruct((T, n), out_dtype)]
    out_specs = [pl.BlockSpec((TM, n), lambda i: (i, 0))]
    if with_h:
        out_shape.append(jax.ShapeDtypeStruct((T, D), F32))
        out_specs.append(pl.BlockSpec((TM, D), lambda i: (i, 0)))
    return pl.pallas_call(
        functools.partial(_modproj_kernel, a=a),
        grid=(NBLK,),
        in_specs=[
            pl.BlockSpec((TM, D), lambda i: (i, 0)),
            pl.BlockSpec((1, D), lambda i: (0, 0)),
            pl.BlockSpec((1, 6, D), lambda i: (_mod_group(i), 0, 0)),
            pl.BlockSpec((D, n), lambda i: (0, 0)),
        ],
        out_specs=out_specs,
        out_shape=out_shape,
        compiler_params=_cparams(("arbitrary",), 40 * 1024 * 1024),
        name="modproj",
    )(x, g.reshape(1, D), mods, w)


def _resid_kernel(x_ref, y_ref, mod_ref, o_ref, *, a):
    o_ref[...] = x_ref[...] + mod_ref[0][a:a + 1] * y_ref[...]


def resid_call(x, y, mods, a):
    return pl.pallas_call(
        functools.partial(_resid_kernel, a=a),
        grid=(NBLK,),
        in_specs=[
            pl.BlockSpec((TM, D), lambda i: (i, 0)),
            pl.BlockSpec((TM, D), lambda i: (i, 0)),
            pl.BlockSpec((1, 6, D), lambda i: (_mod_group(i), 0, 0)),
        ],
        out_specs=pl.BlockSpec((TM, D), lambda i: (i, 0)),
        out_shape=jax.ShapeDtypeStruct((T, D), F32),
        compiler_params=_cparams(("arbitrary",)),
        name="resid",
    )(x, y, mods)


def _outproj_kernel(x_ref, a_ref, a2_ref, r_ref, ggo_ref, b_ref, w_ref, mod_ref, o_ref, *, gla_post):
    if gla_post:
        o = a_ref[...] + a2_ref[...]
        r = r_ref[...]
        parts = []
        for h in range(4):
            seg = _rms(o[:, h * 128:(h + 1) * 128], ggo_ref[...])
            rr = r[:, h * 128:(h + 1) * 128]
            parts.append(seg * (rr / (1.0 + jnp.exp(-rr))))
        mix_a = jnp.concatenate(parts, axis=-1)
    else:
        mix_a = a_ref[...]
    w = w_ref[...]
    y = _dot(mix_a.astype(BF16), w[0:512]) + _dot(b_ref[...].astype(BF16), w[512:1024])
    o_ref[...] = x_ref[...] + mod_ref[0][2:3] * y


def outproj_call(x, mix_a, mix_a2, proj, ra_col, g_go, mix_b, w, mods, gla_post):
    row = lambda i: (i, 0)
    return pl.pallas_call(
        functools.partial(_outproj_kernel, gla_post=gla_post),
        grid=(NBLK,),
        in_specs=[
            pl.BlockSpec((TM, D), row),
            pl.BlockSpec((TM, 512), row),
            pl.BlockSpec((TM, 512), row),
            pl.BlockSpec((TM, 512), lambda i: (i, ra_col)),
            pl.BlockSpec((1, 128), lambda i: (0, 0)),
            pl.BlockSpec((TM, 512), row),
            pl.BlockSpec((D, D), lambda i: (0, 0)),
            pl.BlockSpec((1, 6, D), lambda i: (_mod_group(i), 0, 0)),
        ],
        out_specs=pl.BlockSpec((TM, D), row),
        out_shape=jax.ShapeDtypeStruct((T, D), F32),
        compiler_params=_cparams(("arbitrary",), 40 * 1024 * 1024),
        name="outproj",
    )(x, mix_a, mix_a2, proj, g_go, mix_b, w, mods)


GLA_CHUNK = 64


def _log_sigmoid(z):
    return jnp.minimum(z, 0.0) - jnp.log1p(jnp.exp(-jnp.abs(z)))


def _gla_direction(qk_ref, v_ref, gl_ref, wgu, bgu, st_ref, o_ref, reverse):
    ri = lax.broadcasted_iota(I32, (GLA_CHUNK, GLA_CHUNK), 0)
    ci = lax.broadcasted_iota(I32, (GLA_CHUNK, GLA_CHUNK), 1)
    keep = (ci >= ri) if reverse else (ci <= ri)
    tri = jnp.where(keep, 1.0, 0.0).astype(BF16)
    lane = lax.broadcasted_iota(I32, (GLA_CHUNK, 256), 1)
    lane_s = lax.broadcasted_iota(I32, (128, 256), 1)
    nchunk = TM // GLA_CHUNK
    order = range(nchunk - 1, -1, -1) if reverse else range(nchunk)
    for c in order:
        r0 = c * GLA_CHUNK
        q = qk_ref[r0:r0 + GLA_CHUNK, 0:256] * 0.125
        k = qk_ref[r0:r0 + GLA_CHUNK, 256:512]
        v = v_ref[r0:r0 + GLA_CHUNK, :]
        z = _dot(gl_ref[r0:r0 + GLA_CHUNK, :].astype(BF16), wgu) + bgu
        la = _log_sigmoid(z) * (1.0 / 16.0)
        la_hi, la_lo = _split_bf16(la)
        b = _dot(tri, la_hi) + _dot(tri, la_lo)
        b_tot = b[0:1, :] if reverse else b[GLA_CHUNK - 1:GLA_CHUNK, :]
        q_dec = q * jnp.exp(b)
        k_intra = (k * jnp.exp(-b)).astype(BF16)
        k_state = (k * jnp.exp(b_tot - b)).astype(BF16)
        st = st_ref[...]
        st_b = st.astype(BF16)
        v_b = v.astype(BF16)
        vt = v.T.astype(BF16)
        outs = []
        upd = jnp.zeros((128, 256), F32)
        for h in range(4):
            qm = jnp.where(lane // 64 == h, q_dec, 0.0).astype(BF16)
            att = jnp.where(keep, _dot_nt(qm, k_intra), 0.0)
            o_h = _dot(att.astype(BF16), v_b[:, h * 128:(h + 1) * 128]) + _dot_nt(qm, st_b)
            outs.append(o_h)
            u_h = _dot(vt[h * 128:(h + 1) * 128, :], k_state)
            upd = jnp.where(lane_s // 64 == h, u_h, upd)
        st_ref[...] = st * jnp.exp(b_tot) + upd
        o_ref[r0:r0 + GLA_CHUNK, :] = jnp.concatenate(outs, axis=-1)


def _gla_kernel(qkf, vf, glf, qkb, vb, glb, wgu_ref, bgu_ref, s0_ref, of_ref, ob_ref, sfin_ref, sf_scr, sb_scr):
    i = pl.program_id(0)
    jj = (i - NBLK_P) % SBLK
    first = jnp.logical_or(i < NBLK_P, jj == 0)
    last = jnp.logical_or(i < NBLK_P, jj == SBLK - 1)

    @pl.when(first)
    def _():
        sf_scr[...] = s0_ref[0, 0]
        sb_scr[...] = s0_ref[0, 1]

    _gla_direction(qkf, vf, glf, wgu_ref[0], bgu_ref[0], sf_scr, of_ref, False)
    _gla_direction(qkb, vb, glb, wgu_ref[1], bgu_ref[1], sb_scr, ob_ref, True)

    @pl.when(last)
    def _():
        sfin_ref[0, 0] = sf_scr[...]
        sfin_ref[0, 1] = sb_scr[...]


def _seq_of_block(i):
    return jnp.where(i < NBLK_P, i, NBLK_P + (i - NBLK_P) // SBLK)


def _rev_block(i):
    s = (i - NBLK_P) // SBLK
    jj = (i - NBLK_P) % SBLK
    return jnp.where(i < NBLK_P, i, NBLK_P + s * SBLK + (SBLK - 1 - jj))


def gla_call(proj, wgu_pad, bgu, s0t):
    nseq = N_PROMPT_SEQ + N_SAMPLE_SEQ
    fwd = lambda c: (lambda i: (i, c))
    bwd = lambda c: (lambda i: (_rev_block(i), c))
    return pl.pallas_call(
        _gla_kernel,
        grid=(NBLK,),
        in_specs=[
            pl.BlockSpec((TM, 512), fwd(0)),
            pl.BlockSpec((TM, 512), fwd(1)),
            pl.BlockSpec((TM, 128), fwd(16)),
            pl.BlockSpec((TM, 512), bwd(0)),
            pl.BlockSpec((TM, 512), bwd(1)),
            pl.BlockSpec((TM, 128), bwd(16)),
            pl.BlockSpec((2, 128, 256), lambda i: (0, 0, 0)),
            pl.BlockSpec((2, 1, 256), lambda i: (0, 0, 0)),
            pl.BlockSpec((1, 2, 128, 256), lambda i: (_seq_of_block(i), 0, 0, 0)),
        ],
        out_specs=[
            pl.BlockSpec((TM, 512), fwd(0)),
            pl.BlockSpec((TM, 512), bwd(0)),
            pl.BlockSpec((1, 2, 128, 256), lambda i: (_seq_of_block(i), 0, 0, 0)),
        ],
        out_shape=[
            jax.ShapeDtypeStruct((T, 512), F32),
            jax.ShapeDtypeStruct((T, 512), F32),
            jax.ShapeDtypeStruct((nseq, 2, 128, 256), F32),
        ],
        scratch_shapes=[pltpu.VMEM((128, 256), F32), pltpu.VMEM((128, 256), F32)],
        compiler_params=_cparams(("arbitrary",), 40 * 1024 * 1024),
        name="gla",
    )(proj, proj, proj, proj, proj, proj, wgu_pad, bgu, s0t)


def _rope(x, c, sa, sb):
    return x * c + pltpu.roll(x, 112, 1) * sa + pltpu.roll(x, 16, 1) * sb


def _rope_tables(rot_lanes):
    rows = SAMPLE_LEN // GRID_W
    quarter = 16
    inv = ROPE_BASE ** (-jnp.arange(quarter, dtype=F32) / quarter)
    row = jnp.repeat(jnp.arange(rows, dtype=F32), GRID_W)
    col = jnp.tile(jnp.arange(GRID_W, dtype=F32), rows)
    ar = row[:, None] * inv
    ac = col[:, None] * inv
    ang = jnp.concatenate([ar, ar, ac, ac], axis=-1)
    cos, sin = jnp.cos(ang), jnp.sin(ang)
    seg = (np.arange(64) // 16) % 2
    sa64 = jnp.where(seg == 0, -sin, 0.0)
    sb64 = jnp.where(seg == 1, sin, 0.0)
    if rot_lanes == "upper":
        c = jnp.concatenate([jnp.ones_like(cos), cos], axis=-1)
        sa = jnp.concatenate([jnp.zeros_like(sin), sa64], axis=-1)
        sb = jnp.concatenate([jnp.zeros_like(sin), sb64], axis=-1)
    else:
        c = jnp.concatenate([cos, cos], axis=-1)
        sa = jnp.concatenate([sa64, sa64], axis=-1)
        sb = jnp.concatenate([sb64, sb64], axis=-1)

    def full(t, fill):
        t = jnp.tile(t, (N_SAMPLE_SEQ, 1))
        return jnp.concatenate([jnp.full((T_P, 128), fill, F32), t], axis=0)

    return full(c, 1.0), full(sa, 0.0), full(sb, 0.0)


def _mla_prep_kernel(*refs, with_q, norm_ckv, rope):
    it = iter(refs)
    cq_ref = next(it) if with_q else None
    ckv_ref = next(it)
    kr_ref = next(it)
    gcq_ref = next(it) if with_q else None
    gckv_ref = next(it) if norm_ckv else None
    wuq_ref = next(it) if with_q else None
    wuk_ref = next(it)
    wuv_ref = next(it)
    gqn_ref = next(it) if with_q else None
    gkn_ref = next(it)
    if rope:
        c_ref, sa_ref, sb_ref = next(it), next(it), next(it)
    q_out = next(it) if with_q else None
    k_out = next(it)
    v_out = next(it)
    ckv_out = next(it) if norm_ckv else None
    kr_out = next(it) if norm_ckv else None

    if rope:
        ct, sat, sbt = c_ref[...], sa_ref[...], sb_ref[...]
    ckv = ckv_ref[...]
    if norm_ckv:
        ckv = _rms(ckv, gckv_ref[...])
        ckv_out[...] = ckv
    lane = lax.broadcasted_iota(I32, kr_ref.shape, 1)
    krf = jnp.where(lane >= 64, kr_ref[...], 0.0)
    if norm_ckv:
        kr_out[...] = krf
    ckv_b = ckv.astype(BF16)
    knope = _dot(ckv_b, wuk_ref[...])
    v_out[...] = _dot(ckv_b, wuv_ref[...]).astype(v_out.dtype)
    if with_q:
        qb = _dot(_rms(cq_ref[...], gcq_ref[...]).astype(BF16), wuq_ref[...])
    for h in range(8):
        sl = slice(h * 128, (h + 1) * 128)
        kh = _rms(knope[:, sl] + krf, gkn_ref[...])
        if rope:
            kh = _rope(kh, ct, sat, sbt)
        k_out[:, sl] = kh.astype(k_out.dtype)
        if with_q:
            qh = _rms(qb[:, sl], gqn_ref[...])
            if rope:
                qh = _rope(qh, ct, sat, sbt)
            q_out[:, sl] = qh.astype(q_out.dtype)


def mla_prep_call(proj, gcq, gckv, wuq, wuk_pad, wuv, gqn, gkn, tables):
    row = lambda i: (i, 0)
    const = lambda i: (0, 0)
    return pl.pallas_call(
        functools.partial(_mla_prep_kernel, with_q=True, norm_ckv=True, rope=True),
        grid=(NBLK,),
        in_specs=[
            pl.BlockSpec((TM, 256), lambda i: (i, 6)),
            pl.BlockSpec((TM, 256), lambda i: (i, 7)),
            pl.BlockSpec((TM, 128), lambda i: (i, 16)),
            pl.BlockSpec((1, 256), const),
            pl.BlockSpec((1, 256), const),
            pl.BlockSpec((256, 1024), const),
            pl.BlockSpec((256, 1024), const),
            pl.BlockSpec((256, 512), const),
            pl.BlockSpec((1, 128), const),
            pl.BlockSpec((1, 128), const),
            pl.BlockSpec((TM, 128), row),
            pl.BlockSpec((TM, 128), row),
            pl.BlockSpec((TM, 128), row),
        ],
        out_specs=[
            pl.BlockSpec((TM, 1024), row),
            pl.BlockSpec((TM, 1024), row),
            pl.BlockSpec((TM, 512), row),
            pl.BlockSpec((TM, 256), row),
            pl.BlockSpec((TM, 128), row),
        ],
        out_shape=[
            jax.ShapeDtypeStruct((T, 1024), BF16),
            jax.ShapeDtypeStruct((T, 1024), BF16),
            jax.ShapeDtypeStruct((T, 512), BF16),
            jax.ShapeDtypeStruct((T, 256), F32),
            jax.ShapeDtypeStruct((T, 128), F32),
        ],
        compiler_params=_cparams(("arbitrary",), 40 * 1024 * 1024),
        name="mla_prep",
    )(proj, proj, proj, gcq, gckv, wuq, wuk_pad, wuv, gqn, gkn, *tables)


def mla_ctx_call(ckv_ctx, kr_pad, wuk_pad, wuv, gkn):
    n = ckv_ctx.shape[0]
    row = lambda i: (i, 0)
    const = lambda i: (0, 0)
    return pl.pallas_call(
        functools.partial(_mla_prep_kernel, with_q=False, norm_ckv=False, rope=False),
        grid=(n // TM,),
        in_specs=[
            pl.BlockSpec((TM, 256), row),
            pl.BlockSpec((TM, 128), row),
            pl.BlockSpec((256, 1024), const),
            pl.BlockSpec((256, 512), const),
            pl.BlockSpec((1, 128), const),
        ],
        out_specs=[pl.BlockSpec((TM, 1024), row), pl.BlockSpec((TM, 512), row)],
        out_shape=[jax.ShapeDtypeStruct((n, 1024), BF16), jax.ShapeDtypeStruct((n, 512), BF16)],
        compiler_params=_cparams(("arbitrary",), 40 * 1024 * 1024),
        name="mla_ctx",
    )(ckv_ctx, kr_pad, wuk_pad, wuv, gkn)


def _attn_dense_kernel(q_ref, k_ref, v_ref, *rest, scale, has_ctx):
    if has_ctx:
        kc_ref, vc_ref, o_ref = rest
    else:
        (o_ref,) = rest
    v = v_ref[...]
    outs = []
    for hh in range(2):
        sl = slice(hh * 128, (hh + 1) * 128)
        q = q_ref[:, sl]
        s = _dot_nt(q, k_ref[:, sl]) * scale
        m = jnp.max(s, axis=-1, keepdims=True)
        if has_ctx:
            sc = _dot_nt(q, kc_ref[:, sl]) * scale
            m = jnp.maximum(m, jnp.max(sc, axis=-1, keepdims=True))
        p = jnp.exp(s - m)
        l = jnp.sum(p, axis=-1, keepdims=True)
        o = _dot(p.astype(BF16), v)
        if has_ctx:
            pc = jnp.exp(sc - m)
            l = l + jnp.sum(pc, axis=-1, keepdims=True)
            o = o + _dot(pc.astype(BF16), vc_ref[...])
        outs.append(o / l)
    lane = lax.broadcasted_iota(I32, outs[0].shape, 1)
    o_ref[...] = jnp.where(lane < 64, outs[0], outs[1]).astype(o_ref.dtype)


def mla_attn_prompt_call(qb, kb, vb):
    blk = lambda b, hp: (b, hp)
    return pl.pallas_call(
        functools.partial(_attn_dense_kernel, scale=128 ** -0.5, has_ctx=False),
        grid=(N_PROMPT_SEQ, 4),
        in_specs=[
            pl.BlockSpec((PROMPT_LEN, 256), blk),
            pl.BlockSpec((PROMPT_LEN, 256), blk),
            pl.BlockSpec((PROMPT_LEN, 128), blk),
        ],
        out_specs=pl.BlockSpec((PROMPT_LEN, 128), blk),
        out_shape=jax.ShapeDtypeStruct((T_P, 512), BF16),
        compiler_params=_cparams(("arbitrary", "arbitrary")),
        name="mla_attn_prompt",
    )(qb, kb, vb)


def mla_attn_sample_call(qb, kb, vb, kc, vc):
    tq = 256
    nq = SAMPLE_LEN // tq
    qoff = T_P // tq
    soff = T_P // SAMPLE_LEN
    return pl.pallas_call(
        functools.partial(_attn_dense_kernel, scale=128 ** -0.5, has_ctx=True),
        grid=(N_SAMPLE_SEQ, 4, nq),
        in_specs=[
            pl.BlockSpec((tq, 256), lambda b, hp, qi: (qoff + b * nq + qi, hp)),
            pl.BlockSpec((SAMPLE_LEN, 256), lambda b, hp, qi: (soff + b, hp)),
            pl.BlockSpec((SAMPLE_LEN, 128), lambda b, hp, qi: (soff + b, hp)),
            pl.BlockSpec((PAST, 256), lambda b, hp, qi: (b, hp)),
            pl.BlockSpec((PAST, 128), lambda b, hp, qi: (b, hp)),
        ],
        out_specs=pl.BlockSpec((tq, 128), lambda b, hp, qi: (b * nq + qi, hp)),
        out_shape=jax.ShapeDtypeStruct((T_S, 512), BF16),
        compiler_params=_cparams(("arbitrary", "arbitrary", "arbitrary"), 48 * 1024 * 1024),
        name="mla_attn_sample",
    )(qb, kb, vb, kc, vc)


def _group_mean_sq(x, width):
    gi = lax.broadcasted_iota(I32, (width, width), 0) // 64
    gj = lax.broadcasted_iota(I32, (width, width), 1) // 64
    bd = jnp.where(gi == gj, 1.0, 0.0).astype(BF16)
    hi, lo = _split_bf16(x * x)
    return (_dot(hi, bd) + _dot(lo, bd)) * (1.0 / 64.0)


def _dup64(x, lane):
    r = pltpu.roll(x, 64, 1)
    return jnp.concatenate([jnp.where(lane < 64, x, r), jnp.where(lane >= 64, x, r)], axis=-1)


def _odd_prep_kernel(q_ref, kv_ref, gq_ref, gk_ref, c_ref, sa_ref, sb_ref, q_out, kd_out, vd_out, k_out, v_out):
    ct, sat, sbt = c_ref[...], sa_ref[...], sb_ref[...]
    qd = q_ref[...]
    qn = qd * lax.rsqrt(_group_mean_sq(qd, 512) + EPS)
    for p in range(4):
        sl = slice(p * 128, (p + 1) * 128)
        q_out[:, sl] = _rope(qn[:, sl] * gq_ref[...], ct, sat, sbt).astype(q_out.dtype)
    kd = kv_ref[:, 0:128]
    vd = kv_ref[:, 128:256]
    kn = kd * lax.rsqrt(_group_mean_sq(kd, 128) + EPS) * gk_ref[...]
    k_out[...] = kn
    v_out[...] = vd
    lane = lax.broadcasted_iota(I32, kd.shape, 1)
    kd_out[...] = _dup64(_rope(kn, ct, sat, sbt), lane).astype(kd_out.dtype)
    vd_out[...] = _dup64(vd, lane).astype(vd_out.dtype)


def odd_prep_call(proj, gq2, gk2, tables):
    row = lambda i: (i, 0)
    const = lambda i: (0, 0)
    return pl.pallas_call(
        _odd_prep_kernel,
        grid=(NBLK,),
        in_specs=[
            pl.BlockSpec((TM, 512), lambda i: (i, 1)),
            pl.BlockSpec((TM, 256), lambda i: (i, 4)),
            pl.BlockSpec((1, 128), const),
            pl.BlockSpec((1, 128), const),
            pl.BlockSpec((TM, 128), row),
            pl.BlockSpec((TM, 128), row),
            pl.BlockSpec((TM, 128), row),
        ],
        out_specs=[
            pl.BlockSpec((TM, 512), row),
            pl.BlockSpec((TM, 256), row),
            pl.BlockSpec((TM, 256), row),
            pl.BlockSpec((TM, 128), row),
            pl.BlockSpec((TM, 128), row),
        ],
        out_shape=[
            jax.ShapeDtypeStruct((T, 512), BF16),
            jax.ShapeDtypeStruct((T, 256), BF16),
            jax.ShapeDtypeStruct((T, 256), BF16),
            jax.ShapeDtypeStruct((T, 128), F32),
            jax.ShapeDtypeStruct((T, 128), F32),
        ],
        compiler_params=_cparams(("arbitrary",)),
        name="odd_prep",
    )(proj, proj, gq2, gk2, *tables)


def _sink_heads(q_ref, sink_ref, o_ref, score_fn, value_fn, scale):
    lane = lax.broadcasted_iota(I32, (q_ref.shape[0], 128), 1)
    for p in range(4):
        qp = q_ref[:, p * 128:(p + 1) * 128]
        halves = []
        for par in range(2):
            h = 2 * p + par
            kv = h // 4
            qm = jnp.where((lane >= 64) == (par == 1), qp, jnp.zeros_like(qp))
            ss = score_fn(qm, kv)
            sk = sink_ref[h]
            m = jnp.maximum(functools.reduce(jnp.maximum, [jnp.max(s, axis=-1, keepdims=True) for s in ss]), sk)
            ps = [jnp.exp(s - m) for s in ss]
            l = functools.reduce(jnp.add, [jnp.sum(pp, axis=-1, keepdims=True) for pp in ps]) + jnp.exp(sk - m)
            halves.append(value_fn(ps, kv) / l)
        o_ref[:, p * 128:(p + 1) * 128] = jnp.where(lane < 64, halves[0], halves[1]).astype(o_ref.dtype)


def _win_prompt_kernel(sink_ref, q_ref, k_ref, v_ref, o_ref, *, scale):
    def score_fn(qm, kv):
        return [_dot_nt(qm, k_ref[:, kv * 128:(kv + 1) * 128]) * scale]

    def value_fn(ps, kv):
        return _dot(ps[0].astype(BF16), v_ref[:, kv * 128:(kv + 1) * 128])

    _sink_heads(q_ref, sink_ref, o_ref, score_fn, value_fn, scale)


def win_prompt_call(sink, q, kd, vd):
    blk = lambda b: (b, 0)
    return pl.pallas_call(
        functools.partial(_win_prompt_kernel, scale=64 ** -0.5),
        grid=(N_PROMPT_SEQ,),
        in_specs=[
            pl.BlockSpec(memory_space=pltpu.SMEM),
            pl.BlockSpec((PROMPT_LEN, 512), blk),
            pl.BlockSpec((PROMPT_LEN, 256), blk),
            pl.BlockSpec((PROMPT_LEN, 256), blk),
        ],
        out_specs=pl.BlockSpec((PROMPT_LEN, 512), blk),
        out_shape=jax.ShapeDtypeStruct((T_P, 512), BF16),
        compiler_params=_cparams(("arbitrary",)),
        name="win_prompt",
    )(sink, q, kd, vd)


WIN = 128


def _win_sample_kernel(sink_ref, q_ref, k0, k1, k2, v0, v1, v2, kc_ref, vc_ref, o_ref, *, scale):
    qi = pl.program_id(1)
    start = qi * WIN
    r = lax.broadcasted_iota(I32, (WIN, 3 * WIN), 0)
    c = lax.broadcasted_iota(I32, (WIN, 3 * WIN), 1)
    kpos = start - WIN + c
    valid = (c - r >= 0) & (c - r <= 2 * WIN) & (kpos >= 0) & (kpos < SAMPLE_LEN)

    def score_fn(qm, kv):
        sl = slice(kv * 128, (kv + 1) * 128)
        kloc = jnp.concatenate([k0[:, sl], k1[:, sl], k2[:, sl]], axis=0)
        s_loc = jnp.where(valid, _dot_nt(qm, kloc) * scale, NEG_INF)
        return [s_loc, _dot_nt(qm, kc_ref[:, sl]) * scale]

    def value_fn(ps, kv):
        sl = slice(kv * 128, (kv + 1) * 128)
        vloc = jnp.concatenate([v0[:, sl], v1[:, sl], v2[:, sl]], axis=0)
        return _dot(ps[0].astype(BF16), vloc) + _dot(ps[1].astype(BF16), vc_ref[:, sl])

    _sink_heads(q_ref, sink_ref, o_ref, score_fn, value_fn, scale)


def win_sample_call(sink, q, kpad, vpad, kc, vc):
    nq = SAMPLE_LEN // WIN
    qoff = T_P // WIN
    per = nq + 2
    loc = lambda d: (lambda b, qi: (b * per + qi + d, 0))
    return pl.pallas_call(
        functools.partial(_win_sample_kernel, scale=64 ** -0.5),
        grid=(N_SAMPLE_SEQ, nq),
        in_specs=[
            pl.BlockSpec(memory_space=pltpu.SMEM),
            pl.BlockSpec((WIN, 512), lambda b, qi: (qoff + b * nq + qi, 0)),
            pl.BlockSpec((WIN, 256), loc(0)),
            pl.BlockSpec((WIN, 256), loc(1)),
            pl.BlockSpec((WIN, 256), loc(2)),
            pl.BlockSpec((WIN, 256), loc(0)),
            pl.BlockSpec((WIN, 256), loc(1)),
            pl.BlockSpec((WIN, 256), loc(2)),
            pl.BlockSpec((PAST, 256), lambda b, qi: (b, 0)),
            pl.BlockSpec((PAST, 256), lambda b, qi: (b, 0)),
        ],
        out_specs=pl.BlockSpec((WIN, 512), lambda b, qi: (b * nq + qi, 0)),
        out_shape=jax.ShapeDtypeStruct((T_S, 512), BF16),
        compiler_params=_cparams(("arbitrary", "arbitrary")),
        name="win_sample",
    )(sink, q, kpad, kpad, kpad, vpad, vpad, vpad, kc, vc)


def _pool_kernel(xp_ref, xc_ref, xn_ref, w_ref, sc_ref, o_ref):
    i = pl.program_id(0)
    jj = (i - NBLK_P) % SBLK
    first = jnp.logical_or(i < NBLK_P, jj == 0)
    last = jnp.logical_or(i < NBLK_P, jj == SBLK - 1)
    ext = TM + 2 * POOL_HALO
    r = lax.broadcasted_iota(I32, (TM, ext), 0)
    c = lax.broadcasted_iota(I32, (TM, ext), 1) - POOL_HALO
    ok = jnp.logical_and(jnp.logical_or(c >= 0, jnp.logical_not(first)), jnp.logical_or(c < TM, jnp.logical_not(last)))
    rr = lax.broadcasted_iota(I32, (TM, 1), 0)
    parts = []
    for g, w in enumerate(POOL_WINDOWS):
        sl = slice(g * 128, (g + 1) * 128)
        x = xc_ref[:, sl]
        xe = jnp.concatenate([xp_ref[:, sl], x, xn_ref[:, sl]], axis=0)
        lo = r - w // 2
        band = (c >= lo) & (c < lo + w) & ok
        a = jnp.where(band, 1.0, 0.0).astype(BF16)
        lo1 = rr - w // 2
        lo_c = jnp.where(first, jnp.maximum(lo1, 0), lo1)
        hi_c = jnp.where(last, jnp.minimum(lo1 + w, TM), lo1 + w)
        cnt = (hi_c - lo_c).astype(F32)
        hi_x, lo_x = _split_bf16(xe)
        pooled = (_dot(a, hi_x) + _dot(a, lo_x)) / cnt
        y = _dot((pooled - x).astype(BF16), w_ref[g])
        parts.append(y)
    o_ref[...] = (jnp.concatenate(parts, axis=-1) * sc_ref[...]).astype(o_ref.dtype)


def pool_call(proj, w_pool, scale):
    hb = TM // POOL_HALO
    nh = T // POOL_HALO
    return pl.pallas_call(
        _pool_kernel,
        grid=(NBLK,),
        in_specs=[
            pl.BlockSpec((POOL_HALO, 512), lambda i: (jnp.maximum(i * hb - 1, 0), 0)),
            pl.BlockSpec((TM, 512), lambda i: (i, 0)),
            pl.BlockSpec((POOL_HALO, 512), lambda i: (jnp.minimum((i + 1) * hb, nh - 1), 0)),
            pl.BlockSpec((4, 128, 128), lambda i: (0, 0, 0)),
            pl.BlockSpec((1, 512), lambda i: (0, 0)),
        ],
        out_specs=pl.BlockSpec((TM, 512), lambda i: (i, 0)),
        out_shape=jax.ShapeDtypeStruct((T, 512), BF16),
        compiler_params=_cparams(("arbitrary",)),
        name="pool",
    )(proj, proj, proj, w_pool, scale)


TOPK = 16
TK_TM = 128


def _top16(s, aux=None):
    n, cols = s.shape
    rowi = lax.broadcasted_iota(I32, (n, cols), 0)
    r16 = lax.broadcasted_iota(I32, (TOPK, cols), 0)
    vals = jnp.zeros((TOPK, cols), F32)
    poss = jnp.zeros((TOPK, cols), I32)
    for r in range(TOPK):
        m = jnp.max(s, axis=0, keepdims=True)
        p = jnp.min(jnp.where(s == m, rowi, n), axis=0, keepdims=True)
        hit = rowi == p
        if aux is not None:
            p = jnp.max(jnp.where(hit, aux, -1), axis=0, keepdims=True)
        vals = jnp.where(r16 == r, m, vals)
        poss = jnp.where(r16 == r, p, poss)
        s = jnp.where(hit, -jnp.inf, s)
    return vals, poss


def _peer_topk_kernel(q_ref, sk_ref, idx_ref, gate_ref):
    idx_rows, gate_rows = [], []
    for h in range(8):
        s1 = _dot_nt(sk_ref[2 * h], q_ref[:, (2 * h) * 128:(2 * h + 1) * 128])
        s2 = _dot_nt(sk_ref[2 * h + 1], q_ref[:, (2 * h + 1) * 128:(2 * h + 2) * 128])
        v1, i1 = _top16(s1)
        v2, i2 = _top16(s2)
        cand = jnp.concatenate([v1[a:a + 1] + v2 for a in range(TOPK)], axis=0)
        cand_i = jnp.concatenate([i1[a:a + 1] * 128 + i2 for a in range(TOPK)], axis=0)
        tv, te = _top16(cand, cand_i)
        e = jnp.exp(tv - tv[0:1])
        gate_rows.append(e / jnp.sum(e, axis=0, keepdims=True))
        idx_rows.append(te)
    idx_ref[...] = jnp.concatenate(idx_rows, axis=0).T
    gate_ref[...] = jnp.concatenate(gate_rows, axis=0).T


def peer_topk_call(q, sk):
    row = lambda i: (i, 0)
    return pl.pallas_call(
        _peer_topk_kernel,
        grid=(T // TK_TM,),
        in_specs=[pl.BlockSpec((TK_TM, 2048), row), pl.BlockSpec((16, 128, 128), lambda i: (0, 0, 0))],
        out_specs=[pl.BlockSpec((TK_TM, NPICK), row), pl.BlockSpec((TK_TM, NPICK), row)],
        out_shape=[jax.ShapeDtypeStruct((T, NPICK), I32), jax.ShapeDtypeStruct((T, NPICK), F32)],
        compiler_params=_cparams(("arbitrary",)),
        name="peer_topk",
    )(q, sk)


def _gather_group(idx_ref, tab_ref, stage_ref, base):
    for j in range(PEER_TG):
        for k in range(NPICK):
            e = idx_ref[base + j, k]
            stage_ref[pl.ds(j * 1024 + 8 * k, 8), :] = tab_ref[e]


def _peer_u_epilogue(x_ref, gate_ref, act_ref, stage_ref, base, consts):
    diag, gsum, r8 = consts
    zs = jnp.zeros((8, 1024), F32)
    for j in range(PEER_TG):
        xh, xl = _split_bf16(x_ref[base + j])
        lhs = jnp.concatenate([xh, xl], axis=0)
        z = _dot_nt(lhs, stage_ref[pl.ds(j * 1024, 1024), :])
        zr = jnp.sum(jnp.where(diag, z, 0.0), axis=0, keepdims=True)
        zs = jnp.where(r8 == j, zr, zs)
    zh, zl = _split_bf16(zs)
    sc = _dot(zh, gsum) + _dot(zl, gsum)
    gate = gate_ref[pl.ds(base, PEER_TG), :]
    act = 0.5 * sc * (1.0 + lax.erf(sc * np.float32(1.0 / np.sqrt(2.0)))) * gate
    act_ref[pl.ds(base, PEER_TG), :] = act


def _peer_u_kernel(idx_ref, x_ref, gate_ref, tab_ref, act_ref, stage_a, stage_b):
    col = lax.broadcasted_iota(I32, (16, 1024), 1)
    row = lax.broadcasted_iota(I32, (16, 1024), 0)
    diag = (col % 8) == (row % 8)
    gi = lax.broadcasted_iota(I32, (1024, 128), 0)
    gj = lax.broadcasted_iota(I32, (1024, 128), 1)
    gsum = jnp.where(gi // 8 == gj, 1.0, 0.0).astype(BF16)
    r8 = lax.broadcasted_iota(I32, (8, 1024), 0)
    consts = (diag, gsum, r8)
    ngroup = PEER_TB // PEER_TG
    _gather_group(idx_ref, tab_ref, stage_a, 0)

    def pair(i, carry):
        g = 2 * i
        _gather_group(idx_ref, tab_ref, stage_b, (g + 1) * PEER_TG)
        _peer_u_epilogue(x_ref, gate_ref, act_ref, stage_a, g * PEER_TG, consts)
        _gather_group(idx_ref, tab_ref, stage_a, (g + 2) * PEER_TG)
        _peer_u_epilogue(x_ref, gate_ref, act_ref, stage_b, (g + 1) * PEER_TG, consts)
        return carry

    lax.fori_loop(0, ngroup // 2 - 1, pair, 0)
    _gather_group(idx_ref, tab_ref, stage_b, (ngroup - 1) * PEER_TG)
    _peer_u_epilogue(x_ref, gate_ref, act_ref, stage_a, (ngroup - 2) * PEER_TG, consts)
    _peer_u_epilogue(x_ref, gate_ref, act_ref, stage_b, (ngroup - 1) * PEER_TG, consts)


def _peer_v_epilogue(act_ref, out_ref, stage_ref, base, consts):
    diag, expand = consts
    a = act_ref[pl.ds(base, PEER_TG), :]
    ah, al = _split_bf16(a)
    rep_h = _dot(ah, expand)
    rep_l = _dot(al, expand)
    for j in range(PEER_TG):
        lh = jnp.where(diag, rep_h[j:j + 1, :], 0.0).astype(BF16)
        ll = jnp.where(diag, rep_l[j:j + 1, :], 0.0).astype(BF16)
        s = stage_ref[pl.ds(j * 1024, 1024), :]
        out_ref[base + j] = _dot(lh, s) + _dot(ll, s)


def _peer_v_kernel(idx_ref, act_ref, tab_ref, out_ref, stage_a, stage_b):
    ei = lax.broadcasted_iota(I32, (128, 1024), 0)
    ej = lax.broadcasted_iota(I32, (128, 1024), 1)
    expand = jnp.where(ej // 8 == ei, 1.0, 0.0).astype(BF16)
    col = lax.broadcasted_iota(I32, (8, 1024), 1)
    row = lax.broadcasted_iota(I32, (8, 1024), 0)
    diag = (col % 8) == row
    consts = (diag, expand)
    ngroup = PEER_TB // PEER_TG
    _gather_group(idx_ref, tab_ref, stage_a, 0)

    def pair(i, carry):
        g = 2 * i
        _gather_group(idx_ref, tab_ref, stage_b, (g + 1) * PEER_TG)
        _peer_v_epilogue(act_ref, out_ref, stage_a, g * PEER_TG, consts)
        _gather_group(idx_ref, tab_ref, stage_a, (g + 2) * PEER_TG)
        _peer_v_epilogue(act_ref, out_ref, stage_b, (g + 1) * PEER_TG, consts)
        return carry

    lax.fori_loop(0, ngroup // 2 - 1, pair, 0)
    _gather_group(idx_ref, tab_ref, stage_b, (ngroup - 1) * PEER_TG)
    _peer_v_epilogue(act_ref, out_ref, stage_a, (ngroup - 2) * PEER_TG, consts)
    _peer_v_epilogue(act_ref, out_ref, stage_b, (ngroup - 1) * PEER_TG, consts)


def _stage_scratch():
    return [pltpu.VMEM((PEER_TG * 1024, 128), BF16), pltpu.VMEM((PEER_TG * 1024, 128), BF16)]


def peer_u_call(idx, x3, gate, tab):
    tb = PEER_TB
    return pl.pallas_call(
        _peer_u_kernel,
        grid=(T // tb,),
        in_specs=[
            pl.BlockSpec((tb, NPICK), lambda i: (i, 0), memory_space=pltpu.SMEM),
            pl.BlockSpec((tb, 8, 128), lambda i: (i, 0, 0)),
            pl.BlockSpec((tb, NPICK), lambda i: (i, 0)),
            pl.BlockSpec((P_EXPERTS, 8, 128), lambda i: (0, 0, 0), pipeline_mode=pl.Buffered(1)),
        ],
        out_specs=pl.BlockSpec((tb, NPICK), lambda i: (i, 0)),
        out_shape=jax.ShapeDtypeStruct((T, NPICK), F32),
        scratch_shapes=_stage_scratch(),
        compiler_params=_cparams(("arbitrary",), VMEM_LIMIT),
        name="peer_u",
    )(idx, x3, gate, tab)


def peer_v_call(idx, act, tab):
    tb = PEER_TB
    return pl.pallas_call(
        _peer_v_kernel,
        grid=(T // tb,),
        in_specs=[
            pl.BlockSpec((tb, NPICK), lambda i: (i, 0), memory_space=pltpu.SMEM),
            pl.BlockSpec((tb, NPICK), lambda i: (i, 0)),
            pl.BlockSpec((P_EXPERTS, 8, 128), lambda i: (0, 0, 0), pipeline_mode=pl.Buffered(1)),
        ],
        out_specs=pl.BlockSpec((tb, 8, 128), lambda i: (i, 0, 0)),
        out_shape=jax.ShapeDtypeStruct((T, 8, 128), F32),
        scratch_shapes=_stage_scratch(),
        compiler_params=_cparams(("arbitrary",), VMEM_LIMIT),
        name="peer_v",
    )(idx, act, tab)


def _even_in_weight(w):
    qa, ka, va, ra, glf, glb, cq, ckv, kr = jnp.split(w, np.cumsum([256, 256, 512, 512, 16, 16, 256, 256, 64])[:-1].tolist(), axis=-1)
    pad = jnp.zeros((D, 32), w.dtype)
    return jnp.concatenate([qa, ka, va, ra, cq, ckv, glf, glb, pad, kr], axis=-1).astype(BF16)


def _gate_up_pad(w_gu):
    z = jnp.zeros((2, 128, 256), F32)
    z = z.at[0, 0:16].set(w_gu[0])
    z = z.at[1, 16:32].set(w_gu[1])
    return z.astype(BF16)


def _uk_pad(w_uk):
    w = w_uk.reshape(256, 8, 64)
    return jnp.concatenate([w, jnp.zeros_like(w)], axis=-1).reshape(256, 1024).astype(BF16)


def _dup_heads(x):
    a, b = x[:, :64], x[:, 64:]
    return jnp.concatenate([a, a, b, b], axis=-1).astype(BF16)


def _pad_latent(x):
    x = x.reshape(N_SAMPLE_SEQ, SAMPLE_LEN, 256)
    x = jnp.pad(x, ((0, 0), (WIN, WIN), (0, 0)))
    return x.reshape(N_SAMPLE_SEQ * (SAMPLE_LEN + 2 * WIN), 256)


def kernel(x_prompt, x_sample, state_gla, cache_mla_ckv, cache_mla_krope, cache_win_kv, c, c_ctx, g_norm, w_ada, b_ada, w_in_even, w_gate_up, b_gate_up, g_gla_out, g_mla_cq, g_mla_ckv, w_mla_uq, w_mla_uk, w_mla_uv, g_mla_qn, g_mla_kn, w_out_even, w_in_odd, w_pool, pool_scale, g_win_qn, g_win_kn, win_sink, w_out_odd, peer_wq, peer_subkeys, peer_u, peer_v):
    depth = w_ada.shape[0]
    x = jnp.concatenate([x_prompt.reshape(T_P, D), x_sample.reshape(T_S, D)], axis=0)
    cond8 = jnp.concatenate([c_ctx[None], c, jnp.zeros((3, D), F32)], axis=0)
    mods_all = ada_call(cond8, w_ada, b_ada).reshape(depth, 8, 6, D)
    tables_b = _rope_tables("upper")
    tables_d = _rope_tables("all")

    gla_states, mla_ckv, mla_kr, win_kv = [], [], [], []
    for l in range(depth):
        mods = mods_all[l]
        if l % 2 == 0:
            e = l // 2
            (proj,) = modproj_call(x, g_norm[l, 0], mods, _even_in_weight(w_in_even[e]), 0, F32, False)
            s0 = jnp.concatenate([jnp.zeros((N_PROMPT_SEQ, 2, 4, 64, 128), F32), state_gla[:, e]], axis=0)
            s0t = jnp.swapaxes(s0.reshape(-1, 2, 256, 128), -1, -2)
            o_f, o_b, sfin = gla_call(proj, _gate_up_pad(w_gate_up[e]), b_gate_up[e].reshape(2, 1, 256), s0t)
            sfin = jnp.swapaxes(sfin[:N_PROMPT_SEQ], -1, -2).reshape(N_PROMPT_SEQ, 2, 4, 64, 128)
            gla_states.append(sfin)
            wuk_pad = _uk_pad(w_mla_uk[e])
            wuv = w_mla_uv[e].astype(BF16)
            gkn = g_mla_kn[e].reshape(1, 128)
            qb, kb, vb, ckv_n, krf = mla_prep_call(
                proj, g_mla_cq[e].reshape(1, 256), g_mla_ckv[e].reshape(1, 256), w_mla_uq[e].astype(BF16),
                wuk_pad, wuv, g_mla_qn[e].reshape(1, 128), gkn, tables_b)
            mla_ckv.append(ckv_n[:T_P].reshape(N_PROMPT_SEQ, PROMPT_LEN, 256))
            mla_kr.append(krf[:T_P, 64:].reshape(N_PROMPT_SEQ, PROMPT_LEN, 64))
            kr_ctx = cache_mla_krope[:, e].reshape(N_SAMPLE_SEQ * PAST, 64)
            kr_pad = jnp.concatenate([jnp.zeros_like(kr_ctx), kr_ctx], axis=-1)
            kc, vc = mla_ctx_call(cache_mla_ckv[:, e].reshape(N_SAMPLE_SEQ * PAST, 256), kr_pad, wuk_pad, wuv, gkn)
            ob = jnp.concatenate([mla_attn_prompt_call(qb, kb, vb), mla_attn_sample_call(qb, kb, vb, kc, vc)], axis=0)
            x = outproj_call(x, o_f, o_b, proj, 2, g_gla_out[e].reshape(1, 128), ob, w_out_even[e].astype(BF16), mods, True)
        else:
            o = l // 2
            (proj,) = modproj_call(x, g_norm[l, 0], mods, w_in_odd[o].astype(BF16), 0, F32, False)
            oc = pool_call(proj, w_pool[o].astype(BF16), pool_scale[o].reshape(1, 512))
            gq2 = jnp.tile(g_win_qn[o], 2).reshape(1, 128)
            gk2 = jnp.tile(g_win_kn[o], 2).reshape(1, 128)
            qw, kd, vd, kn, vn = odd_prep_call(proj, gq2, gk2, tables_d)
            k_p = kn[:T_P].reshape(N_PROMPT_SEQ, PROMPT_LEN, 2, 64)
            v_p = vn[:T_P].reshape(N_PROMPT_SEQ, PROMPT_LEN, 2, 64)
            win_kv.append(jnp.stack([k_p, v_p], axis=1))
            kc = _dup_heads(cache_win_kv[:, o, 0].reshape(N_SAMPLE_SEQ * PAST, 128))
            vc = _dup_heads(cache_win_kv[:, o, 1].reshape(N_SAMPLE_SEQ * PAST, 128))
            od = jnp.concatenate([
                win_prompt_call(win_sink[o], qw, kd, vd),
                win_sample_call(win_sink[o], qw, _pad_latent(kd[T_P:]), _pad_latent(vd[T_P:]), kc, vc)], axis=0)
            x = outproj_call(x, oc, oc, proj, 0, jnp.ones((1, 128), F32), od, w_out_odd[o].astype(BF16), mods, False)
        q, h = modproj_call(x, g_norm[l, 1], mods, peer_wq[l].astype(BF16), 3, BF16, True)
        idx, gate = peer_topk_call(q, peer_subkeys[l].reshape(16, 128, 128).astype(BF16))
        tab_u = peer_u[l].astype(BF16).reshape(P_EXPERTS, 8, 128)
        tab_v = peer_v[l].astype(BF16).reshape(P_EXPERTS, 8, 128)
        act = peer_u_call(idx, h.reshape(T, 8, 128), gate, tab_u)
        y = peer_v_call(idx, act, tab_v).reshape(T, D)
        x = resid_call(x, y, mods, 5)

    return (
        x[:T_P].reshape(N_PROMPT_SEQ, PROMPT_LEN, D),
        x[T_P:].reshape(N_SAMPLE_SEQ, SAMPLE_LEN, D),
        jnp.stack(gla_states, axis=1),
        jnp.stack(mla_ckv, axis=1),
        jnp.stack(mla_kr, axis=1),
        jnp.stack(win_kv, axis=1),
    )
```

```python
import functools

import jax
import jax.numpy as jnp
import numpy as np
from jax import lax
from jax.experimental import pallas as pl
from jax.experimental.pallas import tpu as pltpu

F32 = jnp.float32
BF16 = jnp.bfloat16
I32 = jnp.int32

D = 1024
N_PROMPT_SEQ = 32
PROMPT_LEN = 256
N_SAMPLE_SEQ = 4
SAMPLE_LEN = 4096
PAST = 512
T_P = N_PROMPT_SEQ * PROMPT_LEN
T_S = N_SAMPLE_SEQ * SAMPLE_LEN
T = T_P + T_S
GRID_W = 64
EPS = 1e-6
NEG_INF = -1e30
ROPE_BASE = 10000.0

TM = 256
NBLK = T // TM
NBLK_P = T_P // TM
SBLK = SAMPLE_LEN // TM

EVEN_W = 2176
ODD_W = 1280
POOL_WINDOWS = (2, 4, 8, 16)
POOL_HALO = 8

P_EXPERTS = 16384
NPICK = 128
PEER_TB = 64
PEER_TG = 8
VMEM_LIMIT = 56 * 1024 * 1024


def _cparams(sem, vmem=None):
    return pltpu.CompilerParams(dimension_semantics=sem, vmem_limit_bytes=vmem)


def _mod_group(i):
    return jnp.where(i < NBLK_P, 0, 1 + (i - NBLK_P) // SBLK)


def _split_bf16(x):
    hi = x.astype(BF16)
    lo = (x - hi.astype(F32)).astype(BF16)
    return hi, lo


def _dot(a, b):
    return jnp.dot(a, b, preferred_element_type=F32)


def _dot_nt(a, b):
    return lax.dot_general(a, b, (((1,), (1,)), ((), ())), preferred_element_type=F32)


def _rms(x, g):
    ms = jnp.mean(x * x, axis=-1, keepdims=True)
    return x * lax.rsqrt(ms + EPS) * g


def _ada_kernel(c_ref, w_ref, b_ref, o_ref):
    c = c_ref[...]
    s = c / (1.0 + jnp.exp(-c))
    o_ref[0] = _dot(s.astype(BF16), w_ref[0].astype(BF16)) + b_ref[0]


def ada_call(cond8, w_ada, b_ada):
    depth, _, n6 = w_ada.shape
    tn = 1536
    return pl.pallas_call(
        _ada_kernel,
        grid=(depth, n6 // tn),
        in_specs=[
            pl.BlockSpec((8, D), lambda l, j: (0, 0)),
            pl.BlockSpec((1, D, tn), lambda l, j: (l, 0, j)),
            pl.BlockSpec((1, 1, tn), lambda l, j: (l, 0, j)),
        ],
        out_specs=pl.BlockSpec((1, 8, tn), lambda l, j: (l, 0, j)),
        out_shape=jax.ShapeDtypeStruct((depth, 8, n6), F32),
        compiler_params=_cparams(("arbitrary", "arbitrary"), 40 * 1024 * 1024),
        name="ada",
    )(cond8, w_ada, b_ada.reshape(depth, 1, n6))


def _modproj_kernel(x_ref, g_ref, mod_ref, w_ref, o_ref, *h_ref, a):
    x = x_ref[...]
    m = mod_ref[0]
    h = _rms(x, g_ref[...]) * (1.0 + m[a + 1:a + 2]) + m[a:a + 1]
    o_ref[...] = _dot(h.astype(BF16), w_ref[...]).astype(o_ref.dtype)
    if h_ref:
        h_ref[0][...] = h


def modproj_call(x, g, mods, w, a, out_dtype, with_h):
    n = w.shape[1]
    out_shape = [jax.ShapeDtypeStruct((T, n), out_dtype)]
    out_specs = [pl.BlockSpec((TM, n), lambda i: (i, 0))]
    if with_h:
        out_shape.append(jax.ShapeDtypeStruct((T, D), F32))
        out_specs.append(pl.BlockSpec((TM, D), lambda i: (i, 0)))
    return pl.pallas_call(
        functools.partial(_modproj_kernel, a=a),
        grid=(NBLK,),
        in_specs=[
            pl.BlockSpec((TM, D), lambda i: (i, 0)),
            pl.BlockSpec((1, D), lambda i: (0, 0)),
            pl.BlockSpec((1, 6, D), lambda i: (_mod_group(i), 0, 0)),
            pl.BlockSpec((D, n), lambda i: (0, 0)),
        ],
        out_specs=out_specs,
        out_shape=out_shape,
        compiler_params=_cparams(("arbitrary",), 40 * 1024 * 1024),
        name="modproj",
    )(x, g.reshape(1, D), mods, w)


def _resid_kernel(x_ref, y_ref, mod_ref, o_ref, *, a):
    o_ref[...] = x_ref[...] + mod_ref[0][a:a + 1] * y_ref[...]


def resid_call(x, y, mods, a):
    return pl.pallas_call(
        functools.partial(_resid_kernel, a=a),
        grid=(NBLK,),
        in_specs=[
            pl.BlockSpec((TM, D), lambda i: (i, 0)),
            pl.BlockSpec((TM, D), lambda i: (i, 0)),
            pl.BlockSpec((1, 6, D), lambda i: (_mod_group(i), 0, 0)),
        ],
        out_specs=pl.BlockSpec((TM, D), lambda i: (i, 0)),
        out_shape=jax.ShapeDtypeStruct((T, D), F32),
        compiler_params=_cparams(("arbitrary",)),
        name="resid",
    )(x, y, mods)


def _outproj_kernel(x_ref, a_ref, a2_ref, r_ref, ggo_ref, b_ref, w_ref, mod_ref, o_ref, *, gla_post):
    if gla_post:
        o = a_ref[...] + a2_ref[...]
        r = r_ref[...]
        parts = []
        for h in range(4):
            seg = _rms(o[:, h * 128:(h + 1) * 128], ggo_ref[...])
            rr = r[:, h * 128:(h + 1) * 128]
            parts.append(seg * (rr / (1.0 + jnp.exp(-rr))))
        mix_a = jnp.concatenate(parts, axis=-1)
    else:
        mix_a = a_ref[...]
    w = w_ref[...]
    y = _dot(mix_a.astype(BF16), w[0:512]) + _dot(b_ref[...].astype(BF16), w[512:1024])
    o_ref[...] = x_ref[...] + mod_ref[0][2:3] * y


def outproj_call(x, mix_a, mix_a2, proj, ra_col, g_go, mix_b, w, mods, gla_post):
    row = lambda i: (i, 0)
    return pl.pallas_call(
        functools.partial(_outproj_kernel, gla_post=gla_post),
        grid=(NBLK,),
        in_specs=[
            pl.BlockSpec((TM, D), row),
            pl.BlockSpec((TM, 512), row),
            pl.BlockSpec((TM, 512), row),
            pl.BlockSpec((TM, 512), lambda i: (i, ra_col)),
            pl.BlockSpec((1, 128), lambda i: (0, 0)),
            pl.BlockSpec((TM, 512), row),
            pl.BlockSpec((D, D), lambda i: (0, 0)),
            pl.BlockSpec((1, 6, D), lambda i: (_mod_group(i), 0, 0)),
        ],
        out_specs=pl.BlockSpec((TM, D), row),
        out_shape=jax.ShapeDtypeStruct((T, D), F32),
        compiler_params=_cparams(("arbitrary",), 40 * 1024 * 1024),
        name="outproj",
    )(x, mix_a, mix_a2, proj, g_go, mix_b, w, mods)


GLA_CHUNK = 64


def _log_sigmoid(z):
    return jnp.minimum(z, 0.0) - jnp.log1p(jnp.exp(-jnp.abs(z)))


def _gla_direction(qk_ref, v_ref, gl_ref, wgu, bgu, st_ref, o_ref, reverse):
    ri = lax.broadcasted_iota(I32, (GLA_CHUNK, GLA_CHUNK), 0)
    ci = lax.broadcasted_iota(I32, (GLA_CHUNK, GLA_CHUNK), 1)
    keep = (ci >= ri) if reverse else (ci <= ri)
    tri = jnp.where(keep, 1.0, 0.0).astype(BF16)
    lane = lax.broadcasted_iota(I32, (GLA_CHUNK, 256), 1)
    lane_s = lax.broadcasted_iota(I32, (128, 256), 1)
    nchunk = TM // GLA_CHUNK
    order = range(nchunk - 1, -1, -1) if reverse else range(nchunk)
    for c in order:
        r0 = c * GLA_CHUNK
        q = qk_ref[r0:r0 + GLA_CHUNK, 0:256] * 0.125
        k = qk_ref[r0:r0 + GLA_CHUNK, 256:512]
        v = v_ref[r0:r0 + GLA_CHUNK, :]
        z = _dot(gl_ref[r0:r0 + GLA_CHUNK, :].astype(BF16), wgu) + bgu
        la = _log_sigmoid(z) * (1.0 / 16.0)
        la_hi, la_lo = _split_bf16(la)
        b = _dot(tri, la_hi) + _dot(tri, la_lo)
        b_tot = b[0:1, :] if reverse else b[GLA_CHUNK - 1:GLA_CHUNK, :]
        q_dec = q * jnp.exp(b)
        k_intra = (k * jnp.exp(-b)).astype(BF16)
        k_state = (k * jnp.exp(b_tot - b)).astype(BF16)
        st = st_ref[...]
        st_b = st.astype(BF16)
        v_b = v.astype(BF16)
        vt = v.T.astype(BF16)
        outs = []
        upd = jnp.zeros((128, 256), F32)
        for h in range(4):
            qm = jnp.where(lane // 64 == h, q_dec, 0.0).astype(BF16)
            att = jnp.where(keep, _dot_nt(qm, k_intra), 0.0)
            o_h = _dot(att.astype(BF16), v_b[:, h * 128:(h + 1) * 128]) + _dot_nt(qm, st_b)
            outs.append(o_h)
            u_h = _dot(vt[h * 128:(h + 1) * 128, :], k_state)
            upd = jnp.where(lane_s // 64 == h, u_h, upd)
        st_ref[...] = st * jnp.exp(b_tot) + upd
        o_ref[r0:r0 + GLA_CHUNK, :] = jnp.concatenate(outs, axis=-1)


def _gla_kernel(qkf, vf, glf, qkb, vb, glb, wgu_ref, bgu_ref, s0_ref, of_ref, ob_ref, sfin_ref, sf_scr, sb_scr):
    i = pl.program_id(0)
    jj = (i - NBLK_P) % SBLK
    first = jnp.logical_or(i < NBLK_P, jj == 0)
    last = jnp.logical_or(i < NBLK_P, jj == SBLK - 1)

    @pl.when(first)
    def _():
        sf_scr[...] = s0_ref[0, 0]
        sb_scr[...] = s0_ref[0, 1]

    _gla_direction(qkf, vf, glf, wgu_ref[0], bgu_ref[0], sf_scr, of_ref, False)
    _gla_direction(qkb, vb, glb, wgu_ref[1], bgu_ref[1], sb_scr, ob_ref, True)

    @pl.when(last)
    def _():
        sfin_ref[0, 0] = sf_scr[...]
        sfin_ref[0, 1] = sb_scr[...]


def _seq_of_block(i):
    return jnp.where(i < NBLK_P, i, NBLK_P + (i - NBLK_P) // SBLK)


def _rev_block(i):
    s = (i - NBLK_P) // SBLK
    jj = (i - NBLK_P) % SBLK
    return jnp.where(i < NBLK_P, i, NBLK_P + s * SBLK + (SBLK - 1 - jj))


def gla_call(proj, wgu_pad, bgu, s0t):
    nseq = N_PROMPT_SEQ + N_SAMPLE_SEQ
    fwd = lambda c: (lambda i: (i, c))
    bwd = lambda c: (lambda i: (_rev_block(i), c))
    return pl.pallas_call(
        _gla_kernel,
        grid=(NBLK,),
        in_specs=[
            pl.BlockSpec((TM, 512), fwd(0)),
            pl.BlockSpec((TM, 512), fwd(1)),
            pl.BlockSpec((TM, 128), fwd(16)),
            pl.BlockSpec((TM, 512), bwd(0)),
            pl.BlockSpec((TM, 512), bwd(1)),
            pl.BlockSpec((TM, 128), bwd(16)),
            pl.BlockSpec((2, 128, 256), lambda i: (0, 0, 0)),
            pl.BlockSpec((2, 1, 256), lambda i: (0, 0, 0)),
            pl.BlockSpec((1, 2, 128, 256), lambda i: (_seq_of_block(i), 0, 0, 0)),
        ],
        out_specs=[
            pl.BlockSpec((TM, 512), fwd(0)),
            pl.BlockSpec((TM, 512), bwd(0)),
            pl.BlockSpec((1, 2, 128, 256), lambda i: (_seq_of_block(i), 0, 0, 0)),
        ],
        out_shape=[
            jax.ShapeDtypeStruct((T, 512), F32),
            jax.ShapeDtypeStruct((T, 512), F32),
            jax.ShapeDtypeStruct((nseq, 2, 128, 256), F32),
        ],
        scratch_shapes=[pltpu.VMEM((128, 256), F32), pltpu.VMEM((128, 256), F32)],
        compiler_params=_cparams(("arbitrary",), 40 * 1024 * 1024),
        name="gla",
    )(proj, proj, proj, proj, proj, proj, wgu_pad, bgu, s0t)


def _rope(x, c, sa, sb):
    return x * c + pltpu.roll(x, 112, 1) * sa + pltpu.roll(x, 16, 1) * sb


def _rope_tables(rot_lanes):
    rows = SAMPLE_LEN // GRID_W
    quarter = 16
    inv = ROPE_BASE ** (-jnp.arange(quarter, dtype=F32) / quarter)
    row = jnp.repeat(jnp.arange(rows, dtype=F32), GRID_W)
    col = jnp.tile(jnp.arange(GRID_W, dtype=F32), rows)
    ar = row[:, None] * inv
    ac = col[:, None] * inv
    ang = jnp.concatenate([ar, ar, ac, ac], axis=-1)
    cos, sin = jnp.cos(ang), jnp.sin(ang)
    seg = (np.arange(64) // 16) % 2
    sa64 = jnp.where(seg == 0, -sin, 0.0)
    sb64 = jnp.where(seg == 1, sin, 0.0)
    if rot_lanes == "upper":
        c = jnp.concatenate([jnp.ones_like(cos), cos], axis=-1)
        sa = jnp.concatenate([jnp.zeros_like(sin), sa64], axis=-1)
        sb = jnp.concatenate([jnp.zeros_like(sin), sb64], axis=-1)
    else:
        c = jnp.concatenate([cos, cos], axis=-1)
        sa = jnp.concatenate([sa64, sa64], axis=-1)
        sb = jnp.concatenate([sb64, sb64], axis=-1)

    def full(t, fill):
        t = jnp.tile(t, (N_SAMPLE_SEQ, 1))
        return jnp.concatenate([jnp.full((T_P, 128), fill, F32), t], axis=0)

    return full(c, 1.0), full(sa, 0.0), full(sb, 0.0)


def _mla_prep_kernel(*refs, with_q, norm_ckv, rope):
    it = iter(refs)
    cq_ref = next(it) if with_q else None
    ckv_ref = next(it)
    kr_ref = next(it)
    gcq_ref = next(it) if with_q else None
    gckv_ref = next(it) if norm_ckv else None
    wuq_ref = next(it) if with_q else None
    wuk_ref = next(it)
    wuv_ref = next(it)
    gqn_ref = next(it) if with_q else None
    gkn_ref = next(it)
    if rope:
        c_ref, sa_ref, sb_ref = next(it), next(it), next(it)
    q_out = next(it) if with_q else None
    k_out = next(it)
    v_out = next(it)
    ckv_out = next(it) if norm_ckv else None
    kr_out = next(it) if norm_ckv else None

    if rope:
        ct, sat, sbt = c_ref[...], sa_ref[...], sb_ref[...]
    ckv = ckv_ref[...]
    if norm_ckv:
        ckv = _rms(ckv, gckv_ref[...])
        ckv_out[...] = ckv
    lane = lax.broadcasted_iota(I32, kr_ref.shape, 1)
    krf = jnp.where(lane >= 64, kr_ref[...], 0.0)
    if norm_ckv:
        kr_out[...] = krf
    ckv_b = ckv.astype(BF16)
    knope = _dot(ckv_b, wuk_ref[...])
    v_out[...] = _dot(ckv_b, wuv_ref[...]).astype(v_out.dtype)
    if with_q:
        qb = _dot(_rms(cq_ref[...], gcq_ref[...]).astype(BF16), wuq_ref[...])
    for h in range(8):
        sl = slice(h * 128, (h + 1) * 128)
        kh = _rms(knope[:, sl] + krf, gkn_ref[...])
        if rope:
            kh = _rope(kh, ct, sat, sbt)
        k_out[:, sl] = kh.astype(k_out.dtype)
        if with_q:
            qh = _rms(qb[:, sl], gqn_ref[...])
            if rope:
                qh = _rope(qh, ct, sat, sbt)
            q_out[:, sl] = qh.astype(q_out.dtype)


def mla_prep_call(proj, gcq, gckv, wuq, wuk_pad, wuv, gqn, gkn, tables):
    row = lambda i: (i, 0)
    const = lambda i: (0, 0)
    return pl.pallas_call(
        functools.partial(_mla_prep_kernel, with_q=True, norm_ckv=True, rope=True),
        grid=(NBLK,),
        in_specs=[
            pl.BlockSpec((TM, 256), lambda i: (i, 6)),
            pl.BlockSpec((TM, 256), lambda i: (i, 7)),
            pl.BlockSpec((TM, 128), lambda i: (i, 16)),
            pl.BlockSpec((1, 256), const),
            pl.BlockSpec((1, 256), const),
            pl.BlockSpec((256, 1024), const),
            pl.BlockSpec((256, 1024), const),
            pl.BlockSpec((256, 512), const),
            pl.BlockSpec((1, 128), const),
            pl.BlockSpec((1, 128), const),
            pl.BlockSpec((TM, 128), row),
            pl.BlockSpec((TM, 128), row),
            pl.BlockSpec((TM, 128), row),
        ],
        out_specs=[
            pl.BlockSpec((TM, 1024), row),
            pl.BlockSpec((TM, 1024), row),
            pl.BlockSpec((TM, 512), row),
            pl.BlockSpec((TM, 256), row),
            pl.BlockSpec((TM, 128), row),
        ],
        out_shape=[
            jax.ShapeDtypeStruct((T, 1024), BF16),
            jax.ShapeDtypeStruct((T, 1024), BF16),
            jax.ShapeDtypeStruct((T, 512), BF16),
            jax.ShapeDtypeStruct((T, 256), F32),
            jax.ShapeDtypeStruct((T, 128), F32),
        ],
        compiler_params=_cparams(("arbitrary",), 40 * 1024 * 1024),
        name="mla_prep",
    )(proj, proj, proj, gcq, gckv, wuq, wuk_pad, wuv, gqn, gkn, *tables)


def mla_ctx_call(ckv_ctx, kr_pad, wuk_pad, wuv, gkn):
    n = ckv_ctx.shape[0]
    row = lambda i: (i, 0)
    const = lambda i: (0, 0)
    return pl.pallas_call(
        functools.partial(_mla_prep_kernel, with_q=False, norm_ckv=False, rope=False),
        grid=(n // TM,),
        in_specs=[
            pl.BlockSpec((TM, 256), row),
            pl.BlockSpec((TM, 128), row),
            pl.BlockSpec((256, 1024), const),
            pl.BlockSpec((256, 512), const),
            pl.BlockSpec((1, 128), const),
        ],
        out_specs=[pl.BlockSpec((TM, 1024), row), pl.BlockSpec((TM, 512), row)],
        out_shape=[jax.ShapeDtypeStruct((n, 1024), BF16), jax.ShapeDtypeStruct((n, 512), BF16)],
        compiler_params=_cparams(("arbitrary",), 40 * 1024 * 1024),
        name="mla_ctx",
    )(ckv_ctx, kr_pad, wuk_pad, wuv, gkn)


def _attn_dense_kernel(q_ref, k_ref, v_ref, *rest, scale, has_ctx):
    if has_ctx:
        kc_ref, vc_ref, o_ref = rest
    else:
        (o_ref,) = rest
    v = v_ref[...]
    outs = []
    for hh in range(2):
        sl = slice(hh * 128, (hh + 1) * 128)
        q = q_ref[:, sl]
        s = _dot_nt(q, k_ref[:, sl]) * scale
        m = jnp.max(s, axis=-1, keepdims=True)
        if has_ctx:
            sc = _dot_nt(q, kc_ref[:, sl]) * scale
            m = jnp.maximum(m, jnp.max(sc, axis=-1, keepdims=True))
        p = jnp.exp(s - m)
        l = jnp.sum(p, axis=-1, keepdims=True)
        o = _dot(p.astype(BF16), v)
        if has_ctx:
            pc = jnp.exp(sc - m)
            l = l + jnp.sum(pc, axis=-1, keepdims=True)
            o = o + _dot(pc.astype(BF16), vc_ref[...])
        outs.append(o / l)
    lane = lax.broadcasted_iota(I32, outs[0].shape, 1)
    o_ref[...] = jnp.where(lane < 64, outs[0], outs[1]).astype(o_ref.dtype)


def mla_attn_prompt_call(qb, kb, vb):
    blk = lambda b, hp: (b, hp)
    return pl.pallas_call(
        functools.partial(_attn_dense_kernel, scale=128 ** -0.5, has_ctx=False),
        grid=(N_PROMPT_SEQ, 4),
        in_specs=[
            pl.BlockSpec((PROMPT_LEN, 256), blk),
            pl.BlockSpec((PROMPT_LEN, 256), blk),
            pl.BlockSpec((PROMPT_LEN, 128), blk),
        ],
        out_specs=pl.BlockSpec((PROMPT_LEN, 128), blk),
        out_shape=jax.ShapeDtypeStruct((T_P, 512), BF16),
        compiler_params=_cparams(("arbitrary", "arbitrary")),
        name="mla_attn_prompt",
    )(qb, kb, vb)


def mla_attn_sample_call(qb, kb, vb, kc, vc):
    tq = 256
    nq = SAMPLE_LEN // tq
    qoff = T_P // tq
    soff = T_P // SAMPLE_LEN
    return pl.pallas_call(
        functools.partial(_attn_dense_kernel, scale=128 ** -0.5, has_ctx=True),
        grid=(N_SAMPLE_SEQ, 4, nq),
        in_specs=[
            pl.BlockSpec((tq, 256), lambda b, hp, qi: (qoff + b * nq + qi, hp)),
            pl.BlockSpec((SAMPLE_LEN, 256), lambda b, hp, qi: (soff + b, hp)),
            pl.BlockSpec((SAMPLE_LEN, 128), lambda b, hp, qi: (soff + b, hp)),
            pl.BlockSpec((PAST, 256), lambda b, hp, qi: (b, hp)),
            pl.BlockSpec((PAST, 128), lambda b, hp, qi: (b, hp)),
        ],
        out_specs=pl.BlockSpec((tq, 128), lambda b, hp, qi: (b * nq + qi, hp)),
        out_shape=jax.ShapeDtypeStruct((T_S, 512), BF16),
        compiler_params=_cparams(("arbitrary", "arbitrary", "arbitrary"), 48 * 1024 * 1024),
        name="mla_attn_sample",
    )(qb, kb, vb, kc, vc)


def _group_mean_sq(x, width):
    gi = lax.broadcasted_iota(I32, (width, width), 0) // 64
    gj = lax.broadcasted_iota(I32, (width, width), 1) // 64
    bd = jnp.where(gi == gj, 1.0, 0.0).astype(BF16)
    hi, lo = _split_bf16(x * x)
    return (_dot(hi, bd) + _dot(lo, bd)) * (1.0 / 64.0)


def _dup64(x, lane):
    r = pltpu.roll(x, 64, 1)
    return jnp.concatenate([jnp.where(lane < 64, x, r), jnp.where(lane >= 64, x, r)], axis=-1)


def _odd_prep_kernel(q_ref, kv_ref, gq_ref, gk_ref, c_ref, sa_ref, sb_ref, q_out, kd_out, vd_out, k_out, v_out):
    ct, sat, sbt = c_ref[...], sa_ref[...], sb_ref[...]
    qd = q_ref[...]
    qn = qd * lax.rsqrt(_group_mean_sq(qd, 512) + EPS)
    for p in range(4):
        sl = slice(p * 128, (p + 1) * 128)
        q_out[:, sl] = _rope(qn[:, sl] * gq_ref[...], ct, sat, sbt).astype(q_out.dtype)
    kd = kv_ref[:, 0:128]
    vd = kv_ref[:, 128:256]
    kn = kd * lax.rsqrt(_group_mean_sq(kd, 128) + EPS) * gk_ref[...]
    k_out[...] = kn
    v_out[...] = vd
    lane = lax.broadcasted_iota(I32, kd.shape, 1)
    kd_out[...] = _dup64(_rope(kn, ct, sat, sbt), lane).astype(kd_out.dtype)
    vd_out[...] = _dup64(vd, lane).astype(vd_out.dtype)


def odd_prep_call(proj, gq2, gk2, tables):
    row = lambda i: (i, 0)
    const = lambda i: (0, 0)
    return pl.pallas_call(
        _odd_prep_kernel,
        grid=(NBLK,),
        in_specs=[
            pl.BlockSpec((TM, 512), lambda i: (i, 1)),
            pl.BlockSpec((TM, 256), lambda i: (i, 4)),
            pl.BlockSpec((1, 128), const),
            pl.BlockSpec((1, 128), const),
            pl.BlockSpec((TM, 128), row),
            pl.BlockSpec((TM, 128), row),
            pl.BlockSpec((TM, 128), row),
        ],
        out_specs=[
            pl.BlockSpec((TM, 512), row),
            pl.BlockSpec((TM, 256), row),
            pl.BlockSpec((TM, 256), row),
            pl.BlockSpec((TM, 128), row),
            pl.BlockSpec((TM, 128), row),
        ],
        out_shape=[
            jax.ShapeDtypeStruct((T, 512), BF16),
            jax.ShapeDtypeStruct((T, 256), BF16),
            jax.ShapeDtypeStruct((T, 256), BF16),
            jax.ShapeDtypeStruct((T, 128), F32),
            jax.ShapeDtypeStruct((T, 128), F32),
        ],
        compiler_params=_cparams(("arbitrary",)),
        name="odd_prep",
    )(proj, proj, gq2, gk2, *tables)


def _sink_heads(q_ref, sink_ref, o_ref, score_fn, value_fn, scale):
    lane = lax.broadcasted_iota(I32, (q_ref.shape[0], 128), 1)
    for p in range(4):
        qp = q_ref[:, p * 128:(p + 1) * 128]
        halves = []
        for par in range(2):
            h = 2 * p + par
            kv = h // 4
            qm = jnp.where((lane >= 64) == (par == 1), qp, jnp.zeros_like(qp))
            ss = score_fn(qm, kv)
            sk = sink_ref[h]
            m = jnp.maximum(functools.reduce(jnp.maximum, [jnp.max(s, axis=-1, keepdims=True) for s in ss]), sk)
            ps = [jnp.exp(s - m) for s in ss]
            l = functools.reduce(jnp.add, [jnp.sum(pp, axis=-1, keepdims=True) for pp in ps]) + jnp.exp(sk - m)
            halves.append(value_fn(ps, kv) / l)
        o_ref[:, p * 128:(p + 1) * 128] = jnp.where(lane < 64, halves[0], halves[1]).astype(o_ref.dtype)


def _win_prompt_kernel(sink_ref, q_ref, k_ref, v_ref, o_ref, *, scale):
    def score_fn(qm, kv):
        return [_dot_nt(qm, k_ref[:, kv * 128:(kv + 1) * 128]) * scale]

    def value_fn(ps, kv):
        return _dot(ps[0].astype(BF16), v_ref[:, kv * 128:(kv + 1) * 128])

    _sink_heads(q_ref, sink_ref, o_ref, score_fn, value_fn, scale)


def win_prompt_call(sink, q, kd, vd):
    blk = lambda b: (b, 0)
    return pl.pallas_call(
        functools.partial(_win_prompt_kernel, scale=64 ** -0.5),
        grid=(N_PROMPT_SEQ,),
        in_specs=[
            pl.BlockSpec(memory_space=pltpu.SMEM),
            pl.BlockSpec((PROMPT_LEN, 512), blk),
            pl.BlockSpec((PROMPT_LEN, 256), blk),
            pl.BlockSpec((PROMPT_LEN, 256), blk),
        ],
        out_specs=pl.BlockSpec((PROMPT_LEN, 512), blk),
        out_shape=jax.ShapeDtypeStruct((T_P, 512), BF16),
        compiler_params=_cparams(("arbitrary",)),
        name="win_prompt",
    )(sink, q, kd, vd)


WIN = 128


def _win_sample_kernel(sink_ref, q_ref, k0, k1, k2, v0, v1, v2, kc_ref, vc_ref, o_ref, *, scale):
    qi = pl.program_id(1)
    start = qi * WIN
    r = lax.broadcasted_iota(I32, (WIN, 3 * WIN), 0)
    c = lax.broadcasted_iota(I32, (WIN, 3 * WIN), 1)
    kpos = start - WIN + c
    valid = (c - r >= 0) & (c - r <= 2 * WIN) & (kpos >= 0) & (kpos < SAMPLE_LEN)

    def score_fn(qm, kv):
        sl = slice(kv * 128, (kv + 1) * 128)
        kloc = jnp.concatenate([k0[:, sl], k1[:, sl], k2[:, sl]], axis=0)
        s_loc = jnp.where(valid, _dot_nt(qm, kloc) * scale, NEG_INF)
        return [s_loc, _dot_nt(qm, kc_ref[:, sl]) * scale]

    def value_fn(ps, kv):
        sl = slice(kv * 128, (kv + 1) * 128)
        vloc = jnp.concatenate([v0[:, sl], v1[:, sl], v2[:, sl]], axis=0)
        return _dot(ps[0].astype(BF16), vloc) + _dot(ps[1].astype(BF16), vc_ref[:, sl])

    _sink_heads(q_ref, sink_ref, o_ref, score_fn, value_fn, scale)


def win_sample_call(sink, q, kpad, vpad, kc, vc):
    nq = SAMPLE_LEN // WIN
    qoff = T_P // WIN
    per = nq + 2
    loc = lambda d: (lambda b, qi: (b * per + qi + d, 0))
    return pl.pallas_call(
        functools.partial(_win_sample_kernel, scale=64 ** -0.5),
        grid=(N_SAMPLE_SEQ, nq),
        in_specs=[
            pl.BlockSpec(memory_space=pltpu.SMEM),
            pl.BlockSpec((WIN, 512), lambda b, qi: (qoff + b * nq + qi, 0)),
            pl.BlockSpec((WIN, 256), loc(0)),
            pl.BlockSpec((WIN, 256), loc(1)),
            pl.BlockSpec((WIN, 256), loc(2)),
            pl.BlockSpec((WIN, 256), loc(0)),
            pl.BlockSpec((WIN, 256), loc(1)),
            pl.BlockSpec((WIN, 256), loc(2)),
            pl.BlockSpec((PAST, 256), lambda b, qi: (b, 0)),
            pl.BlockSpec((PAST, 256), lambda b, qi: (b, 0)),
        ],
        out_specs=pl.BlockSpec((WIN, 512), lambda b, qi: (b * nq + qi, 0)),
        out_shape=jax.ShapeDtypeStruct((T_S, 512), BF16),
        compiler_params=_cparams(("arbitrary", "arbitrary")),
        name="win_sample",
    )(sink, q, kpad, kpad, kpad, vpad, vpad, vpad, kc, vc)


def _pool_kernel(xp_ref, xc_ref, xn_ref, w_ref, sc_ref, o_ref):
    i = pl.program_id(0)
    jj = (i - NBLK_P) % SBLK
    first = jnp.logical_or(i < NBLK_P, jj == 0)
    last = jnp.logical_or(i < NBLK_P, jj == SBLK - 1)
    ext = TM + 2 * POOL_HALO
    r = lax.broadcasted_iota(I32, (TM, ext), 0)
    c = lax.broadcasted_iota(I32, (TM, ext), 1) - POOL_HALO
    ok = jnp.logical_and(jnp.logical_or(c >= 0, jnp.logical_not(first)), jnp.logical_or(c < TM, jnp.logical_not(last)))
    rr = lax.broadcasted_iota(I32, (TM, 1), 0)
    parts = []
    for g, w in enumerate(POOL_WINDOWS):
        sl = slice(g * 128, (g + 1) * 128)
        x = xc_ref[:, sl]
        xe = jnp.concatenate([xp_ref[:, sl], x, xn_ref[:, sl]], axis=0)
        lo = r - w // 2
        band = (c >= lo) & (c < lo + w) & ok
        a = jnp.where(band, 1.0, 0.0).astype(BF16)
        lo1 = rr - w // 2
        lo_c = jnp.where(first, jnp.maximum(lo1, 0), lo1)
        hi_c = jnp.where(last, jnp.minimum(lo1 + w, TM), lo1 + w)
        cnt = (hi_c - lo_c).astype(F32)
        hi_x, lo_x = _split_bf16(xe)
        pooled = (_dot(a, hi_x) + _dot(a, lo_x)) / cnt
        y = _dot((pooled - x).astype(BF16), w_ref[g])
        parts.append(y)
    o_ref[...] = (jnp.concatenate(parts, axis=-1) * sc_ref[...]).astype(o_ref.dtype)


def pool_call(proj, w_pool, scale):
    hb = TM // POOL_HALO
    nh = T // POOL_HALO
    return pl.pallas_call(
        _pool_kernel,
        grid=(NBLK,),
        in_specs=[
            pl.BlockSpec((POOL_HALO, 512), lambda i: (jnp.maximum(i * hb - 1, 0), 0)),
            pl.BlockSpec((TM, 512), lambda i: (i, 0)),
            pl.BlockSpec((POOL_HALO, 512), lambda i: (jnp.minimum((i + 1) * hb, nh - 1), 0)),
            pl.BlockSpec((4, 128, 128), lambda i: (0, 0, 0)),
            pl.BlockSpec((1, 512), lambda i: (0, 0)),
        ],
        out_specs=pl.BlockSpec((TM, 512), lambda i: (i, 0)),
        out_shape=jax.ShapeDtypeStruct((T, 512), BF16),
        compiler_params=_cparams(("arbitrary",)),
        name="pool",
    )(proj, proj, proj, w_pool, scale)


TOPK = 16
TK_TM = 128


def _top16(s, key, aux=None):
    cols = s.shape[1]
    r16 = lax.broadcasted_iota(I32, (TOPK, cols), 0)
    vals = jnp.zeros((TOPK, cols), F32)
    outs = jnp.zeros((TOPK, cols), F32)
    for r in range(TOPK):
        m = jnp.max(s, axis=0, keepdims=True)
        p = jnp.min(jnp.where(s == m, key, np.float32(1e9)), axis=0, keepdims=True)
        hit = key == p
        o = p if aux is None else jnp.max(jnp.where(hit, aux, -1.0), axis=0, keepdims=True)
        vals = jnp.where(r16 == r, m, vals)
        outs = jnp.where(r16 == r, o, outs)
        s = jnp.where(hit, -jnp.inf, s)
    return vals, outs


def _pair_candidates(v1, i1, v2, i2):
    cols = v1.shape[1]
    n8 = lax.broadcasted_iota(I32, (8, cols), 0)
    n16 = lax.broadcasted_iota(I32, (16, cols), 0)
    f8, f16 = n8.astype(F32), n16.astype(F32)
    sc, pos, ex = [], [], []

    def add(score, position, expert, mask):
        sc.append(score if mask is None else jnp.where(mask, score, -jnp.inf))
        pos.append(position)
        ex.append(expert)

    add(v1[0:1] + v2, f16, i1[0:1] * 128.0 + i2, None)
    add(v1[1:2] + v2[0:8], 16.0 + f8, i1[1:2] * 128.0 + i2[0:8], None)
    add(v1[2:3] + v2[0:8], 32.0 + f8, i1[2:3] * 128.0 + i2[0:8], n8 < 5)
    add(v1[3:4] + v2[0:8], 48.0 + f8, i1[3:4] * 128.0 + i2[0:8], n8 < 4)
    add(v1 + v2[0:1], f16 * 16.0, i1 * 128.0 + i2[0:1], n16 >= 4)
    add(v1[0:8] + v2[1:2], f8 * 16.0 + 1.0, i1[0:8] * 128.0 + i2[1:2], n8 >= 4)
    add(v1[0:8] + v2[2:3], f8 * 16.0 + 2.0, i1[0:8] * 128.0 + i2[2:3], n8 == 4)
    return jnp.concatenate(sc, axis=0), jnp.concatenate(pos, axis=0), jnp.concatenate(ex, axis=0)


def _peer_topk_kernel(q_ref, sk_ref, idx_ref, gate_ref):
    rowf = lax.broadcasted_iota(I32, (128, q_ref.shape[0]), 0).astype(F32)
    idx_rows, gate_rows = [], []
    for h in range(8):
        s1 = _dot_nt(sk_ref[2 * h], q_ref[:, (2 * h) * 128:(2 * h + 1) * 128])
        s2 = _dot_nt(sk_ref[2 * h + 1], q_ref[:, (2 * h + 1) * 128:(2 * h + 2) * 128])
        v1, i1 = _top16(s1, rowf)
        v2, i2 = _top16(s2, rowf)
        cand, pos, expert = _pair_candidates(v1, i1, v2, i2)
        tv, te = _top16(cand, pos, expert)
        e = jnp.exp(tv - tv[0:1])
        gate_rows.append(e / jnp.sum(e, axis=0, keepdims=True))
        idx_rows.append(te)
    idx_ref[...] = (jnp.concatenate(idx_rows, axis=0).T * 4.0).astype(I32)
    gate_ref[...] = jnp.concatenate(gate_rows, axis=0).T


def peer_topk_call(q, sk):
    row = lambda i: (i, 0)
    return pl.pallas_call(
        _peer_topk_kernel,
        grid=(T // TK_TM,),
        in_specs=[pl.BlockSpec((TK_TM, 2048), row), pl.BlockSpec((16, 128, 128), lambda i: (0, 0, 0))],
        out_specs=[pl.BlockSpec((TK_TM, NPICK), row), pl.BlockSpec((TK_TM, NPICK), row)],
        out_shape=[jax.ShapeDtypeStruct((T, NPICK), I32), jax.ShapeDtypeStruct((T, NPICK), F32)],
        compiler_params=_cparams(("arbitrary",)),
        name="peer_topk",
    )(q, sk)


def _gather_group(idx_ref, tab_ref, stage_ref, g):
    for j in range(PEER_TG):
        for k in range(NPICK):
            e4 = pl.multiple_of(idx_ref[g * PEER_TG + j, k], 4)
            stage_ref[pl.ds(j * 512 + 4 * k, 4), :] = tab_ref[pl.ds(e4, 4), :]


def _staged_tiles(stage_ref, j):
    return pltpu.bitcast(stage_ref[pl.ds(j * 512, 512), :], BF16)


def _peer_u_epilogue(x_ref, gate_ref, act_ref, stage_ref, base, consts):
    diag, gsum, r8 = consts
    zs = jnp.zeros((8, 1024), F32)
    for j in range(PEER_TG):
        xh, xl = _split_bf16(x_ref[base + j])
        lhs = jnp.concatenate([xh, xl], axis=0)
        z = _dot_nt(lhs, _staged_tiles(stage_ref, j))
        zr = jnp.sum(jnp.where(diag, z, 0.0), axis=0, keepdims=True)
        zs = jnp.where(r8 == j, zr, zs)
    zh, zl = _split_bf16(zs)
    sc = _dot(zh, gsum) + _dot(zl, gsum)
    gate = gate_ref[pl.ds(base, PEER_TG), :]
    act = 0.5 * sc * (1.0 + lax.erf(sc * np.float32(1.0 / np.sqrt(2.0)))) * gate
    act_ref[pl.ds(base, PEER_TG), :] = act


def _peer_u_kernel(idx_ref, x_ref, gate_ref, tab_ref, act_ref, stage_a, stage_b):
    col = lax.broadcasted_iota(I32, (16, 1024), 1)
    row = lax.broadcasted_iota(I32, (16, 1024), 0)
    diag = (col % 8) == (row % 8)
    gi = lax.broadcasted_iota(I32, (1024, 128), 0)
    gj = lax.broadcasted_iota(I32, (1024, 128), 1)
    gsum = jnp.where(gi // 8 == gj, 1.0, 0.0).astype(BF16)
    r8 = lax.broadcasted_iota(I32, (8, 1024), 0)
    consts = (diag, gsum, r8)
    stages = (stage_a, stage_b)
    ngroup = PEER_TB // PEER_TG
    _gather_group(idx_ref, tab_ref, stages[0], 0)
    for g in range(ngroup):
        if g + 1 < ngroup:
            _gather_group(idx_ref, tab_ref, stages[(g + 1) % 2], g + 1)
        _peer_u_epilogue(x_ref, gate_ref, act_ref, stages[g % 2], g * PEER_TG, consts)


def _peer_v_epilogue(act_ref, out_ref, stage_ref, base, consts):
    diag, expand = consts
    a = act_ref[pl.ds(base, PEER_TG), :]
    ah, al = _split_bf16(a)
    rep_h = _dot(ah, expand)
    rep_l = _dot(al, expand)
    for j in range(PEER_TG):
        lh = jnp.where(diag, rep_h[j:j + 1, :], 0.0).astype(BF16)
        ll = jnp.where(diag, rep_l[j:j + 1, :], 0.0).astype(BF16)
        o = _dot(jnp.concatenate([lh, ll], axis=0), _staged_tiles(stage_ref, j))
        out_ref[base + j] = o[0:8] + o[8:16]


def _peer_v_kernel(idx_ref, act_ref, tab_ref, out_ref, stage_a, stage_b):
    ei = lax.broadcasted_iota(I32, (128, 1024), 0)
    ej = lax.broadcasted_iota(I32, (128, 1024), 1)
    expand = jnp.where(ej // 8 == ei, 1.0, 0.0).astype(BF16)
    col = lax.broadcasted_iota(I32, (8, 1024), 1)
    row = lax.broadcasted_iota(I32, (8, 1024), 0)
    diag = (col % 8) == row
    consts = (diag, expand)
    stages = (stage_a, stage_b)
    ngroup = PEER_TB // PEER_TG
    _gather_group(idx_ref, tab_ref, stages[0], 0)
    for g in range(ngroup):
        if g + 1 < ngroup:
            _gather_group(idx_ref, tab_ref, stages[(g + 1) % 2], g + 1)
        _peer_v_epilogue(act_ref, out_ref, stages[g % 2], g * PEER_TG, consts)


def _stage_scratch():
    return [pltpu.VMEM((PEER_TG * 512, 128), jnp.uint32), pltpu.VMEM((PEER_TG * 512, 128), jnp.uint32)]


def _pack_table(tab):
    t = tab.astype(BF16).reshape(P_EXPERTS, 4, 2, 128)
    t = jnp.swapaxes(t, -1, -2)
    return lax.bitcast_convert_type(t, jnp.uint32).reshape(P_EXPERTS * 4, 128)


def peer_u_call(idx, x3, gate, tab):
    tb = PEER_TB
    return pl.pallas_call(
        _peer_u_kernel,
        grid=(T // tb,),
        in_specs=[
            pl.BlockSpec((tb, NPICK), lambda i: (i, 0), memory_space=pltpu.SMEM),
            pl.BlockSpec((tb, 8, 128), lambda i: (i, 0, 0)),
            pl.BlockSpec((tb, NPICK), lambda i: (i, 0)),
            pl.BlockSpec((P_EXPERTS * 4, 128), lambda i: (0, 0), pipeline_mode=pl.Buffered(1)),
        ],
        out_specs=pl.BlockSpec((tb, NPICK), lambda i: (i, 0)),
        out_shape=jax.ShapeDtypeStruct((T, NPICK), F32),
        scratch_shapes=_stage_scratch(),
        compiler_params=_cparams(("arbitrary",), VMEM_LIMIT),
        name="peer_u",
    )(idx, x3, gate, tab)


def peer_v_call(idx, act, tab):
    tb = PEER_TB
    return pl.pallas_call(
        _peer_v_kernel,
        grid=(T // tb,),
        in_specs=[
            pl.BlockSpec((tb, NPICK), lambda i: (i, 0), memory_space=pltpu.SMEM),
            pl.BlockSpec((tb, NPICK), lambda i: (i, 0)),
            pl.BlockSpec((P_EXPERTS * 4, 128), lambda i: (0, 0), pipeline_mode=pl.Buffered(1)),
        ],
        out_specs=pl.BlockSpec((tb, 8, 128), lambda i: (i, 0, 0)),
        out_shape=jax.ShapeDtypeStruct((T, 8, 128), F32),
        scratch_shapes=_stage_scratch(),
        compiler_params=_cparams(("arbitrary",), VMEM_LIMIT),
        name="peer_v",
    )(idx, act, tab)


def _even_in_weight(w):
    qa, ka, va, ra, glf, glb, cq, ckv, kr = jnp.split(w, np.cumsum([256, 256, 512, 512, 16, 16, 256, 256, 64])[:-1].tolist(), axis=-1)
    pad = jnp.zeros((D, 32), w.dtype)
    return jnp.concatenate([qa, ka, va, ra, cq, ckv, glf, glb, pad, kr], axis=-1).astype(BF16)


def _gate_up_pad(w_gu):
    z = jnp.zeros((2, 128, 256), F32)
    z = z.at[0, 0:16].set(w_gu[0])
    z = z.at[1, 16:32].set(w_gu[1])
    return z.astype(BF16)


def _uk_pad(w_uk):
    w = w_uk.reshape(256, 8, 64)
    return jnp.concatenate([w, jnp.zeros_like(w)], axis=-1).reshape(256, 1024).astype(BF16)


def _dup_heads(x):
    a, b = x[:, :64], x[:, 64:]
    return jnp.concatenate([a, a, b, b], axis=-1).astype(BF16)


def _pad_latent(x):
    x = x.reshape(N_SAMPLE_SEQ, SAMPLE_LEN, 256)
    x = jnp.pad(x, ((0, 0), (WIN, WIN), (0, 0)))
    return x.reshape(N_SAMPLE_SEQ * (SAMPLE_LEN + 2 * WIN), 256)


def kernel(x_prompt, x_sample, state_gla, cache_mla_ckv, cache_mla_krope, cache_win_kv, c, c_ctx, g_norm, w_ada, b_ada, w_in_even, w_gate_up, b_gate_up, g_gla_out, g_mla_cq, g_mla_ckv, w_mla_uq, w_mla_uk, w_mla_uv, g_mla_qn, g_mla_kn, w_out_even, w_in_odd, w_pool, pool_scale, g_win_qn, g_win_kn, win_sink, w_out_odd, peer_wq, peer_subkeys, peer_u, peer_v):
    depth = w_ada.shape[0]
    x = jnp.concatenate([x_prompt.reshape(T_P, D), x_sample.reshape(T_S, D)], axis=0)
    cond8 = jnp.concatenate([c_ctx[None], c, jnp.zeros((3, D), F32)], axis=0)
    mods_all = ada_call(cond8, w_ada, b_ada).reshape(depth, 8, 6, D)
    tables_b = _rope_tables("upper")
    tables_d = _rope_tables("all")

    gla_states, mla_ckv, mla_kr, win_kv = [], [], [], []
    for l in range(depth):
        mods = mods_all[l]
        if l % 2 == 0:
            e = l // 2
            (proj,) = modproj_call(x, g_norm[l, 0], mods, _even_in_weight(w_in_even[e]), 0, F32, False)
            s0 = jnp.concatenate([jnp.zeros((N_PROMPT_SEQ, 2, 4, 64, 128), F32), state_gla[:, e]], axis=0)
            s0t = jnp.swapaxes(s0.reshape(-1, 2, 256, 128), -1, -2)
            o_f, o_b, sfin = gla_call(proj, _gate_up_pad(w_gate_up[e]), b_gate_up[e].reshape(2, 1, 256), s0t)
            sfin = jnp.swapaxes(sfin[:N_PROMPT_SEQ], -1, -2).reshape(N_PROMPT_SEQ, 2, 4, 64, 128)
            gla_states.append(sfin)
            wuk_pad = _uk_pad(w_mla_uk[e])
            wuv = w_mla_uv[e].astype(BF16)
            gkn = g_mla_kn[e].reshape(1, 128)
            qb, kb, vb, ckv_n, krf = mla_prep_call(
                proj, g_mla_cq[e].reshape(1, 256), g_mla_ckv[e].reshape(1, 256), w_mla_uq[e].astype(BF16),
                wuk_pad, wuv, g_mla_qn[e].reshape(1, 128), gkn, tables_b)
            mla_ckv.append(ckv_n[:T_P].reshape(N_PROMPT_SEQ, PROMPT_LEN, 256))
            mla_kr.append(krf[:T_P, 64:].reshape(N_PROMPT_SEQ, PROMPT_LEN, 64))
            kr_ctx = cache_mla_krope[:, e].reshape(N_SAMPLE_SEQ * PAST, 64)
            kr_pad = jnp.concatenate([jnp.zeros_like(kr_ctx), kr_ctx], axis=-1)
            kc, vc = mla_ctx_call(cache_mla_ckv[:, e].reshape(N_SAMPLE_SEQ * PAST, 256), kr_pad, wuk_pad, wuv, gkn)
            ob = jnp.concatenate([mla_attn_prompt_call(qb, kb, vb), mla_attn_sample_call(qb, kb, vb, kc, vc)], axis=0)
            x = outproj_call(x, o_f, o_b, proj, 2, g_gla_out[e].reshape(1, 128), ob, w_out_even[e].astype(BF16), mods, True)
        else:
            o = l // 2
            (proj,) = modproj_call(x, g_norm[l, 0], mods, w_in_odd[o].astype(BF16), 0, F32, False)
            oc = pool_call(proj, w_pool[o].astype(BF16), pool_scale[o].reshape(1, 512))
            gq2 = jnp.tile(g_win_qn[o], 2).reshape(1, 128)
            gk2 = jnp.tile(g_win_kn[o], 2).reshape(1, 128)
            qw, kd, vd, kn, vn = odd_prep_call(proj, gq2, gk2, tables_d)
            k_p = kn[:T_P].reshape(N_PROMPT_SEQ, PROMPT_LEN, 2, 64)
            v_p = vn[:T_P].reshape(N_PROMPT_SEQ, PROMPT_LEN, 2, 64)
            win_kv.append(jnp.stack([k_p, v_p], axis=1))
            kc = _dup_heads(cache_win_kv[:, o, 0].reshape(N_SAMPLE_SEQ * PAST, 128))
            vc = _dup_heads(cache_win_kv[:, o, 1].reshape(N_SAMPLE_SEQ * PAST, 128))
            od = jnp.concatenate([
                win_prompt_call(win_sink[o], qw, kd, vd),
                win_sample_call(win_sink[o], qw, _pad_latent(kd[T_P:]), _pad_latent(vd[T_P:]), kc, vc)], axis=0)
            x = outproj_call(x, oc, oc, proj, 0, jnp.ones((1, 128), F32), od, w_out_odd[o].astype(BF16), mods, False)
        q, h = modproj_call(x, g_norm[l, 1], mods, peer_wq[l].astype(BF16), 3, BF16, True)
        idx, gate = peer_topk_call(q, peer_subkeys[l].reshape(16, 128, 128).astype(BF16))
        tab_u = _pack_table(peer_u[l])
        tab_v = _pack_table(peer_v[l])
        act = peer_u_call(idx, h.reshape(T, 8, 128), gate, tab_u)
        y = peer_v_call(idx, act, tab_v).reshape(T, D)
        x = resid_call(x, y, mods, 5)

    return (
        x[:T_P].reshape(N_PROMPT_SEQ, PROMPT_LEN, D),
        x[T_P:].reshape(N_SAMPLE_SEQ, SAMPLE_LEN, D),
        jnp.stack(gla_states, axis=1),
        jnp.stack(mla_ckv, axis=1),
        jnp.stack(mla_kr, axis=1),
        jnp.stack(win_kv, axis=1),
    )
```

```python
import functools

import jax
import jax.numpy as jnp
import numpy as np
from jax import lax
from jax.experimental import pallas as pl
from jax.experimental.pallas import tpu as pltpu

F32 = jnp.float32
BF16 = jnp.bfloat16
I32 = jnp.int32

D = 1024
N_PROMPT_SEQ = 32
PROMPT_LEN = 256
N_SAMPLE_SEQ = 4
SAMPLE_LEN = 4096
PAST = 512
T_P = N_PROMPT_SEQ * PROMPT_LEN
T_S = N_SAMPLE_SEQ * SAMPLE_LEN
T = T_P + T_S
GRID_W = 64
EPS = 1e-6
NEG_INF = -1e30
ROPE_BASE = 10000.0

TM = 256
NBLK = T // TM
NBLK_P = T_P // TM
SBLK = SAMPLE_LEN // TM

EVEN_W = 2176
ODD_W = 1280
POOL_WINDOWS = (2, 4, 8, 16)
POOL_HALO = 8

P_EXPERTS = 16384
NPICK = 128
PEER_TB = 64
PEER_TG = 8
NSTAGE = 1
VMEM_LIMIT = 56 * 1024 * 1024


def _cparams(sem, vmem=None):
    return pltpu.CompilerParams(dimension_semantics=sem, vmem_limit_bytes=vmem)


def _mod_group(i):
    return jnp.where(i < NBLK_P, 0, 1 + (i - NBLK_P) // SBLK)


def _split_bf16(x):
    hi = x.astype(BF16)
    lo = (x - hi.astype(F32)).astype(BF16)
    return hi, lo


def _dot(a, b):
    return jnp.dot(a, b, preferred_element_type=F32)


def _dot_nt(a, b):
    return lax.dot_general(a, b, (((1,), (1,)), ((), ())), preferred_element_type=F32)


def _rms(x, g):
    ms = jnp.mean(x * x, axis=-1, keepdims=True)
    return x * lax.rsqrt(ms + EPS) * g


def _ada_kernel(c_ref, w_ref, b_ref, o_ref):
    c = c_ref[...]
    s = c / (1.0 + jnp.exp(-c))
    o_ref[0] = _dot(s.astype(BF16), w_ref[0].astype(BF16)) + b_ref[0]


def ada_call(cond8, w_ada, b_ada):
    depth, _, n6 = w_ada.shape
    tn = 1536
    return pl.pallas_call(
        _ada_kernel,
        grid=(depth, n6 // tn),
        in_specs=[
            pl.BlockSpec((8, D), lambda l, j: (0, 0)),
            pl.BlockSpec((1, D, tn), lambda l, j: (l, 0, j)),
            pl.BlockSpec((1, 1, tn), lambda l, j: (l, 0, j)),
        ],
        out_specs=pl.BlockSpec((1, 8, tn), lambda l, j: (l, 0, j)),
        out_shape=jax.ShapeDtypeStruct((depth, 8, n6), F32),
        compiler_params=_cparams(("arbitrary", "arbitrary"), 40 * 1024 * 1024),
        name="ada",
    )(cond8, w_ada, b_ada.reshape(depth, 1, n6))


def _modproj_kernel(x_ref, g_ref, mod_ref, w_ref, o_ref, *h_ref, a):
    x = x_ref[...]
    m = mod_ref[0]
    h = _rms(x, g_ref[...]) * (1.0 + m[a + 1:a + 2]) + m[a:a + 1]
    o_ref[...] = _dot(h.astype(BF16), w_ref[...]).astype(o_ref.dtype)
    if h_ref:
        h_ref[0][...] = h


def modproj_call(x, g, mods, w, a, out_dtype, with_h):
    n = w.shape[1]
    out_shape = [jax.ShapeDtypeStruct((T, n), out_dtype)]
    out_specs = [pl.BlockSpec((TM, n), lambda i: (i, 0))]
    if with_h:
        out_shape.append(jax.ShapeDtypeStruct((T, D), F32))
        out_specs.append(pl.BlockSpec((TM, D), lambda i: (i, 0)))
    return pl.pallas_call(
        functools.partial(_modproj_kernel, a=a),
        grid=(NBLK,),
        in_specs=[
            pl.BlockSpec((TM, D), lambda i: (i, 0)),
            pl.BlockSpec((1, D), lambda i: (0, 0)),
            pl.BlockSpec((1, 6, D), lambda i: (_mod_group(i), 0, 0)),
            pl.BlockSpec((D, n), lambda i: (0, 0)),
        ],
        out_specs=out_specs,
        out_shape=out_shape,
        compiler_params=_cparams(("arbitrary",), 40 * 1024 * 1024),
        name="modproj",
    )(x, g.reshape(1, D), mods, w)


def _resid_kernel(x_ref, y_ref, mod_ref, o_ref, *, a):
    o_ref[...] = x_ref[...] + mod_ref[0][a:a + 1] * y_ref[...]


def resid_call(x, y, mods, a):
    return pl.pallas_call(
        functools.partial(_resid_kernel, a=a),
        grid=(NBLK,),
        in_specs=[
            pl.BlockSpec((TM, D), lambda i: (i, 0)),
            pl.BlockSpec((TM, D), lambda i: (i, 0)),
            pl.BlockSpec((1, 6, D), lambda i: (_mod_group(i), 0, 0)),
        ],
        out_specs=pl.BlockSpec((TM, D), lambda i: (i, 0)),
        out_shape=jax.ShapeDtypeStruct((T, D), F32),
        compiler_params=_cparams(("arbitrary",)),
        name="resid",
    )(x, y, mods)


def _outproj_kernel(x_ref, a_ref, a2_ref, r_ref, ggo_ref, b_ref, w_ref, mod_ref, o_ref, *, gla_post):
    if gla_post:
        o = a_ref[...] + a2_ref[...]
        r = r_ref[...]
        parts = []
        for h in range(4):
            seg = _rms(o[:, h * 128:(h + 1) * 128], ggo_ref[...])
            rr = r[:, h * 128:(h + 1) * 128]
            parts.append(seg * (rr / (1.0 + jnp.exp(-rr))))
        mix_a = jnp.concatenate(parts, axis=-1)
    else:
        mix_a = a_ref[...]
    w = w_ref[...]
    y = _dot(mix_a.astype(BF16), w[0:512]) + _dot(b_ref[...].astype(BF16), w[512:1024])
    o_ref[...] = x_ref[...] + mod_ref[0][2:3] * y


def outproj_call(x, mix_a, mix_a2, proj, ra_col, g_go, mix_b, w, mods, gla_post):
    row = lambda i: (i, 0)
    return pl.pallas_call(
        functools.partial(_outproj_kernel, gla_post=gla_post),
        grid=(NBLK,),
        in_specs=[
            pl.BlockSpec((TM, D), row),
            pl.BlockSpec((TM, 512), row),
            pl.BlockSpec((TM, 512), row),
            pl.BlockSpec((TM, 512), lambda i: (i, ra_col)),
            pl.BlockSpec((1, 128), lambda i: (0, 0)),
            pl.BlockSpec((TM, 512), row),
            pl.BlockSpec((D, D), lambda i: (0, 0)),
            pl.BlockSpec((1, 6, D), lambda i: (_mod_group(i), 0, 0)),
        ],
        out_specs=pl.BlockSpec((TM, D), row),
        out_shape=jax.ShapeDtypeStruct((T, D), F32),
        compiler_params=_cparams(("arbitrary",), 40 * 1024 * 1024),
        name="outproj",
    )(x, mix_a, mix_a2, proj, g_go, mix_b, w, mods)


GLA_CHUNK = 64


def _log_sigmoid(z):
    return jnp.minimum(z, 0.0) - jnp.log1p(jnp.exp(-jnp.abs(z)))


def _gla_direction(qk_ref, v_ref, gl_ref, wgu, bgu, st_ref, o_ref, reverse):
    ri = lax.broadcasted_iota(I32, (GLA_CHUNK, GLA_CHUNK), 0)
    ci = lax.broadcasted_iota(I32, (GLA_CHUNK, GLA_CHUNK), 1)
    keep = (ci >= ri) if reverse else (ci <= ri)
    tri = jnp.where(keep, 1.0, 0.0).astype(BF16)
    lane = lax.broadcasted_iota(I32, (GLA_CHUNK, 256), 1)
    lane_s = lax.broadcasted_iota(I32, (128, 256), 1)
    nchunk = TM // GLA_CHUNK
    order = range(nchunk - 1, -1, -1) if reverse else range(nchunk)
    for c in order:
        r0 = c * GLA_CHUNK
        q = qk_ref[r0:r0 + GLA_CHUNK, 0:256] * 0.125
        k = qk_ref[r0:r0 + GLA_CHUNK, 256:512]
        v = v_ref[r0:r0 + GLA_CHUNK, :]
        z = _dot(gl_ref[r0:r0 + GLA_CHUNK, :].astype(BF16), wgu) + bgu
        la = _log_sigmoid(z) * (1.0 / 16.0)
        la_hi, la_lo = _split_bf16(la)
        b = _dot(tri, la_hi) + _dot(tri, la_lo)
        b_tot = b[0:1, :] if reverse else b[GLA_CHUNK - 1:GLA_CHUNK, :]
        q_dec = q * jnp.exp(b)
        k_intra = (k * jnp.exp(-b)).astype(BF16)
        k_state = (k * jnp.exp(b_tot - b)).astype(BF16)
        st = st_ref[...]
        st_b = st.astype(BF16)
        v_b = v.astype(BF16)
        vt = v.T.astype(BF16)
        outs = []
        upd = jnp.zeros((128, 256), F32)
        for h in range(4):
            qm = jnp.where(lane // 64 == h, q_dec, 0.0).astype(BF16)
            att = jnp.where(keep, _dot_nt(qm, k_intra), 0.0)
            o_h = _dot(att.astype(BF16), v_b[:, h * 128:(h + 1) * 128]) + _dot_nt(qm, st_b)
            outs.append(o_h)
            u_h = _dot(vt[h * 128:(h + 1) * 128, :], k_state)
            upd = jnp.where(lane_s // 64 == h, u_h, upd)
        st_ref[...] = st * jnp.exp(b_tot) + upd
        o_ref[r0:r0 + GLA_CHUNK, :] = jnp.concatenate(outs, axis=-1)


def _gla_kernel(qkf, vf, glf, qkb, vb, glb, wgu_ref, bgu_ref, s0_ref, of_ref, ob_ref, sfin_ref, sf_scr, sb_scr):
    i = pl.program_id(0)
    jj = (i - NBLK_P) % SBLK
    first = jnp.logical_or(i < NBLK_P, jj == 0)
    last = jnp.logical_or(i < NBLK_P, jj == SBLK - 1)

    @pl.when(first)
    def _():
        sf_scr[...] = s0_ref[0, 0]
        sb_scr[...] = s0_ref[0, 1]

    _gla_direction(qkf, vf, glf, wgu_ref[0], bgu_ref[0], sf_scr, of_ref, False)
    _gla_direction(qkb, vb, glb, wgu_ref[1], bgu_ref[1], sb_scr, ob_ref, True)

    @pl.when(last)
    def _():
        sfin_ref[0, 0] = sf_scr[...]
        sfin_ref[0, 1] = sb_scr[...]


def _seq_of_block(i):
    return jnp.where(i < NBLK_P, i, NBLK_P + (i - NBLK_P) // SBLK)


def _rev_block(i):
    s = (i - NBLK_P) // SBLK
    jj = (i - NBLK_P) % SBLK
    return jnp.where(i < NBLK_P, i, NBLK_P + s * SBLK + (SBLK - 1 - jj))


def gla_call(proj, wgu_pad, bgu, s0t):
    nseq = N_PROMPT_SEQ + N_SAMPLE_SEQ
    fwd = lambda c: (lambda i: (i, c))
    bwd = lambda c: (lambda i: (_rev_block(i), c))
    return pl.pallas_call(
        _gla_kernel,
        grid=(NBLK,),
        in_specs=[
            pl.BlockSpec((TM, 512), fwd(0)),
            pl.BlockSpec((TM, 512), fwd(1)),
            pl.BlockSpec((TM, 128), fwd(16)),
            pl.BlockSpec((TM, 512), bwd(0)),
            pl.BlockSpec((TM, 512), bwd(1)),
            pl.BlockSpec((TM, 128), bwd(16)),
            pl.BlockSpec((2, 128, 256), lambda i: (0, 0, 0)),
            pl.BlockSpec((2, 1, 256), lambda i: (0, 0, 0)),
            pl.BlockSpec((1, 2, 128, 256), lambda i: (_seq_of_block(i), 0, 0, 0)),
        ],
        out_specs=[
            pl.BlockSpec((TM, 512), fwd(0)),
            pl.BlockSpec((TM, 512), bwd(0)),
            pl.BlockSpec((1, 2, 128, 256), lambda i: (_seq_of_block(i), 0, 0, 0)),
        ],
        out_shape=[
            jax.ShapeDtypeStruct((T, 512), F32),
            jax.ShapeDtypeStruct((T, 512), F32),
            jax.ShapeDtypeStruct((nseq, 2, 128, 256), F32),
        ],
        scratch_shapes=[pltpu.VMEM((128, 256), F32), pltpu.VMEM((128, 256), F32)],
        compiler_params=_cparams(("arbitrary",), 40 * 1024 * 1024),
        name="gla",
    )(proj, proj, proj, proj, proj, proj, wgu_pad, bgu, s0t)


def _rope(x, c, sa, sb):
    return x * c + pltpu.roll(x, 112, 1) * sa + pltpu.roll(x, 16, 1) * sb


def _rope_tables(rot_lanes):
    rows = SAMPLE_LEN // GRID_W
    quarter = 16
    inv = ROPE_BASE ** (-jnp.arange(quarter, dtype=F32) / quarter)
    row = jnp.repeat(jnp.arange(rows, dtype=F32), GRID_W)
    col = jnp.tile(jnp.arange(GRID_W, dtype=F32), rows)
    ar = row[:, None] * inv
    ac = col[:, None] * inv
    ang = jnp.concatenate([ar, ar, ac, ac], axis=-1)
    cos, sin = jnp.cos(ang), jnp.sin(ang)
    seg = (np.arange(64) // 16) % 2
    sa64 = jnp.where(seg == 0, -sin, 0.0)
    sb64 = jnp.where(seg == 1, sin, 0.0)
    if rot_lanes == "upper":
        c = jnp.concatenate([jnp.ones_like(cos), cos], axis=-1)
        sa = jnp.concatenate([jnp.zeros_like(sin), sa64], axis=-1)
        sb = jnp.concatenate([jnp.zeros_like(sin), sb64], axis=-1)
    else:
        c = jnp.concatenate([cos, cos], axis=-1)
        sa = jnp.concatenate([sa64, sa64], axis=-1)
        sb = jnp.concatenate([sb64, sb64], axis=-1)

    def full(t, fill):
        t = jnp.tile(t, (N_SAMPLE_SEQ, 1))
        return jnp.concatenate([jnp.full((T_P, 128), fill, F32), t], axis=0)

    return full(c, 1.0), full(sa, 0.0), full(sb, 0.0)


def _mla_prep_kernel(*refs, with_q, norm_ckv, rope):
    it = iter(refs)
    cq_ref = next(it) if with_q else None
    ckv_ref = next(it)
    kr_ref = next(it)
    gcq_ref = next(it) if with_q else None
    gckv_ref = next(it) if norm_ckv else None
    wuq_ref = next(it) if with_q else None
    wuk_ref = next(it)
    wuv_ref = next(it)
    gqn_ref = next(it) if with_q else None
    gkn_ref = next(it)
    if rope:
        c_ref, sa_ref, sb_ref = next(it), next(it), next(it)
    q_out = next(it) if with_q else None
    k_out = next(it)
    v_out = next(it)
    ckv_out = next(it) if norm_ckv else None
    kr_out = next(it) if norm_ckv else None

    if rope:
        ct, sat, sbt = c_ref[...], sa_ref[...], sb_ref[...]
    ckv = ckv_ref[...]
    if norm_ckv:
        ckv = _rms(ckv, gckv_ref[...])
        ckv_out[...] = ckv
    lane = lax.broadcasted_iota(I32, kr_ref.shape, 1)
    krf = jnp.where(lane >= 64, kr_ref[...], 0.0)
    if norm_ckv:
        kr_out[...] = krf
    ckv_b = ckv.astype(BF16)
    knope = _dot(ckv_b, wuk_ref[...])
    v_out[...] = _dot(ckv_b, wuv_ref[...]).astype(v_out.dtype)
    if with_q:
        qb = _dot(_rms(cq_ref[...], gcq_ref[...]).astype(BF16), wuq_ref[...])
    for h in range(8):
        sl = slice(h * 128, (h + 1) * 128)
        kh = _rms(knope[:, sl] + krf, gkn_ref[...])
        if rope:
            kh = _rope(kh, ct, sat, sbt)
        k_out[:, sl] = kh.astype(k_out.dtype)
        if with_q:
            qh = _rms(qb[:, sl], gqn_ref[...])
            if rope:
                qh = _rope(qh, ct, sat, sbt)
            q_out[:, sl] = qh.astype(q_out.dtype)


def mla_prep_call(proj, gcq, gckv, wuq, wuk_pad, wuv, gqn, gkn, tables):
    row = lambda i: (i, 0)
    const = lambda i: (0, 0)
    return pl.pallas_call(
        functools.partial(_mla_prep_kernel, with_q=True, norm_ckv=True, rope=True),
        grid=(NBLK,),
        in_specs=[
            pl.BlockSpec((TM, 256), lambda i: (i, 6)),
            pl.BlockSpec((TM, 256), lambda i: (i, 7)),
            pl.BlockSpec((TM, 128), lambda i: (i, 16)),
            pl.BlockSpec((1, 256), const),
            pl.BlockSpec((1, 256), const),
            pl.BlockSpec((256, 1024), const),
            pl.BlockSpec((256, 1024), const),
            pl.BlockSpec((256, 512), const),
            pl.BlockSpec((1, 128), const),
            pl.BlockSpec((1, 128), const),
            pl.BlockSpec((TM, 128), row),
            pl.BlockSpec((TM, 128), row),
            pl.BlockSpec((TM, 128), row),
        ],
        out_specs=[
            pl.BlockSpec((TM, 1024), row),
            pl.BlockSpec((TM, 1024), row),
            pl.BlockSpec((TM, 512), row),
            pl.BlockSpec((TM, 256), row),
            pl.BlockSpec((TM, 128), row),
        ],
        out_shape=[
            jax.ShapeDtypeStruct((T, 1024), BF16),
            jax.ShapeDtypeStruct((T, 1024), BF16),
            jax.ShapeDtypeStruct((T, 512), BF16),
            jax.ShapeDtypeStruct((T, 256), F32),
            jax.ShapeDtypeStruct((T, 128), F32),
        ],
        compiler_params=_cparams(("arbitrary",), 40 * 1024 * 1024),
        name="mla_prep",
    )(proj, proj, proj, gcq, gckv, wuq, wuk_pad, wuv, gqn, gkn, *tables)


def mla_ctx_call(ckv_ctx, kr_pad, wuk_pad, wuv, gkn):
    n = ckv_ctx.shape[0]
    row = lambda i: (i, 0)
    const = lambda i: (0, 0)
    return pl.pallas_call(
        functools.partial(_mla_prep_kernel, with_q=False, norm_ckv=False, rope=False),
        grid=(n // TM,),
        in_specs=[
            pl.BlockSpec((TM, 256), row),
            pl.BlockSpec((TM, 128), row),
            pl.BlockSpec((256, 1024), const),
            pl.BlockSpec((256, 512), const),
            pl.BlockSpec((1, 128), const),
        ],
        out_specs=[pl.BlockSpec((TM, 1024), row), pl.BlockSpec((TM, 512), row)],
        out_shape=[jax.ShapeDtypeStruct((n, 1024), BF16), jax.ShapeDtypeStruct((n, 512), BF16)],
        compiler_params=_cparams(("arbitrary",), 40 * 1024 * 1024),
        name="mla_ctx",
    )(ckv_ctx, kr_pad, wuk_pad, wuv, gkn)


def _attn_dense_kernel(q_ref, k_ref, v_ref, *rest, scale, has_ctx):
    if has_ctx:
        kc_ref, vc_ref, o_ref = rest
    else:
        (o_ref,) = rest
    v = v_ref[...]
    outs = []
    for hh in range(2):
        sl = slice(hh * 128, (hh + 1) * 128)
        q = q_ref[:, sl]
        s = _dot_nt(q, k_ref[:, sl]) * scale
        m = jnp.max(s, axis=-1, keepdims=True)
        if has_ctx:
            sc = _dot_nt(q, kc_ref[:, sl]) * scale
            m = jnp.maximum(m, jnp.max(sc, axis=-1, keepdims=True))
        p = jnp.exp(s - m)
        l = jnp.sum(p, axis=-1, keepdims=True)
        o = _dot(p.astype(BF16), v)
        if has_ctx:
            pc = jnp.exp(sc - m)
            l = l + jnp.sum(pc, axis=-1, keepdims=True)
            o = o + _dot(pc.astype(BF16), vc_ref[...])
        outs.append(o / l)
    lane = lax.broadcasted_iota(I32, outs[0].shape, 1)
    o_ref[...] = jnp.where(lane < 64, outs[0], outs[1]).astype(o_ref.dtype)


def mla_attn_prompt_call(qb, kb, vb):
    blk = lambda b, hp: (b, hp)
    return pl.pallas_call(
        functools.partial(_attn_dense_kernel, scale=128 ** -0.5, has_ctx=False),
        grid=(N_PROMPT_SEQ, 4),
        in_specs=[
            pl.BlockSpec((PROMPT_LEN, 256), blk),
            pl.BlockSpec((PROMPT_LEN, 256), blk),
            pl.BlockSpec((PROMPT_LEN, 128), blk),
        ],
        out_specs=pl.BlockSpec((PROMPT_LEN, 128), blk),
        out_shape=jax.ShapeDtypeStruct((T_P, 512), BF16),
        compiler_params=_cparams(("arbitrary", "arbitrary")),
        name="mla_attn_prompt",
    )(qb, kb, vb)


def mla_attn_sample_call(qb, kb, vb, kc, vc):
    tq = 256
    nq = SAMPLE_LEN // tq
    qoff = T_P // tq
    soff = T_P // SAMPLE_LEN
    return pl.pallas_call(
        functools.partial(_attn_dense_kernel, scale=128 ** -0.5, has_ctx=True),
        grid=(N_SAMPLE_SEQ, 4, nq),
        in_specs=[
            pl.BlockSpec((tq, 256), lambda b, hp, qi: (qoff + b * nq + qi, hp)),
            pl.BlockSpec((SAMPLE_LEN, 256), lambda b, hp, qi: (soff + b, hp)),
            pl.BlockSpec((SAMPLE_LEN, 128), lambda b, hp, qi: (soff + b, hp)),
            pl.BlockSpec((PAST, 256), lambda b, hp, qi: (b, hp)),
            pl.BlockSpec((PAST, 128), lambda b, hp, qi: (b, hp)),
        ],
        out_specs=pl.BlockSpec((tq, 128), lambda b, hp, qi: (b * nq + qi, hp)),
        out_shape=jax.ShapeDtypeStruct((T_S, 512), BF16),
        compiler_params=_cparams(("arbitrary", "arbitrary", "arbitrary"), 48 * 1024 * 1024),
        name="mla_attn_sample",
    )(qb, kb, vb, kc, vc)


def _group_mean_sq(x, width):
    gi = lax.broadcasted_iota(I32, (width, width), 0) // 64
    gj = lax.broadcasted_iota(I32, (width, width), 1) // 64
    bd = jnp.where(gi == gj, 1.0, 0.0).astype(BF16)
    hi, lo = _split_bf16(x * x)
    return (_dot(hi, bd) + _dot(lo, bd)) * (1.0 / 64.0)


def _dup64(x, lane):
    r = pltpu.roll(x, 64, 1)
    return jnp.concatenate([jnp.where(lane < 64, x, r), jnp.where(lane >= 64, x, r)], axis=-1)


def _odd_prep_kernel(q_ref, kv_ref, gq_ref, gk_ref, c_ref, sa_ref, sb_ref, q_out, kd_out, vd_out, k_out, v_out):
    ct, sat, sbt = c_ref[...], sa_ref[...], sb_ref[...]
    qd = q_ref[...]
    qn = qd * lax.rsqrt(_group_mean_sq(qd, 512) + EPS)
    for p in range(4):
        sl = slice(p * 128, (p + 1) * 128)
        q_out[:, sl] = _rope(qn[:, sl] * gq_ref[...], ct, sat, sbt).astype(q_out.dtype)
    kd = kv_ref[:, 0:128]
    vd = kv_ref[:, 128:256]
    kn = kd * lax.rsqrt(_group_mean_sq(kd, 128) + EPS) * gk_ref[...]
    k_out[...] = kn
    v_out[...] = vd
    lane = lax.broadcasted_iota(I32, kd.shape, 1)
    kd_out[...] = _dup64(_rope(kn, ct, sat, sbt), lane).astype(kd_out.dtype)
    vd_out[...] = _dup64(vd, lane).astype(vd_out.dtype)


def odd_prep_call(proj, gq2, gk2, tables):
    row = lambda i: (i, 0)
    const = lambda i: (0, 0)
    return pl.pallas_call(
        _odd_prep_kernel,
        grid=(NBLK,),
        in_specs=[
            pl.BlockSpec((TM, 512), lambda i: (i, 1)),
            pl.BlockSpec((TM, 256), lambda i: (i, 4)),
            pl.BlockSpec((1, 128), const),
            pl.BlockSpec((1, 128), const),
            pl.BlockSpec((TM, 128), row),
            pl.BlockSpec((TM, 128), row),
            pl.BlockSpec((TM, 128), row),
        ],
        out_specs=[
            pl.BlockSpec((TM, 512), row),
            pl.BlockSpec((TM, 256), row),
            pl.BlockSpec((TM, 256), row),
            pl.BlockSpec((TM, 128), row),
            pl.BlockSpec((TM, 128), row),
        ],
        out_shape=[
            jax.ShapeDtypeStruct((T, 512), BF16),
            jax.ShapeDtypeStruct((T, 256), BF16),
            jax.ShapeDtypeStruct((T, 256), BF16),
            jax.ShapeDtypeStruct((T, 128), F32),
            jax.ShapeDtypeStruct((T, 128), F32),
        ],
        compiler_params=_cparams(("arbitrary",)),
        name="odd_prep",
    )(proj, proj, gq2, gk2, *tables)


def _sink_heads(q_ref, sink_ref, o_ref, score_fn, value_fn, scale):
    lane = lax.broadcasted_iota(I32, (q_ref.shape[0], 128), 1)
    for p in range(4):
        qp = q_ref[:, p * 128:(p + 1) * 128]
        halves = []
        for par in range(2):
            h = 2 * p + par
            kv = h // 4
            qm = jnp.where((lane >= 64) == (par == 1), qp, jnp.zeros_like(qp))
            ss = score_fn(qm, kv)
            sk = sink_ref[h]
            m = jnp.maximum(functools.reduce(jnp.maximum, [jnp.max(s, axis=-1, keepdims=True) for s in ss]), sk)
            ps = [jnp.exp(s - m) for s in ss]
            l = functools.reduce(jnp.add, [jnp.sum(pp, axis=-1, keepdims=True) for pp in ps]) + jnp.exp(sk - m)
            halves.append(value_fn(ps, kv) / l)
        o_ref[:, p * 128:(p + 1) * 128] = jnp.where(lane < 64, halves[0], halves[1]).astype(o_ref.dtype)


def _win_prompt_kernel(sink_ref, q_ref, k_ref, v_ref, o_ref, *, scale):
    def score_fn(qm, kv):
        return [_dot_nt(qm, k_ref[:, kv * 128:(kv + 1) * 128]) * scale]

    def value_fn(ps, kv):
        return _dot(ps[0].astype(BF16), v_ref[:, kv * 128:(kv + 1) * 128])

    _sink_heads(q_ref, sink_ref, o_ref, score_fn, value_fn, scale)


def win_prompt_call(sink, q, kd, vd):
    blk = lambda b: (b, 0)
    return pl.pallas_call(
        functools.partial(_win_prompt_kernel, scale=64 ** -0.5),
        grid=(N_PROMPT_SEQ,),
        in_specs=[
            pl.BlockSpec(memory_space=pltpu.SMEM),
            pl.BlockSpec((PROMPT_LEN, 512), blk),
            pl.BlockSpec((PROMPT_LEN, 256), blk),
            pl.BlockSpec((PROMPT_LEN, 256), blk),
        ],
        out_specs=pl.BlockSpec((PROMPT_LEN, 512), blk),
        out_shape=jax.ShapeDtypeStruct((T_P, 512), BF16),
        compiler_params=_cparams(("arbitrary",)),
        name="win_prompt",
    )(sink, q, kd, vd)


WIN = 128


def _win_sample_kernel(sink_ref, q_ref, k0, k1, k2, v0, v1, v2, kc_ref, vc_ref, o_ref, *, scale):
    qi = pl.program_id(1)
    start = qi * WIN
    r = lax.broadcasted_iota(I32, (WIN, 3 * WIN), 0)
    c = lax.broadcasted_iota(I32, (WIN, 3 * WIN), 1)
    kpos = start - WIN + c
    valid = (c - r >= 0) & (c - r <= 2 * WIN) & (kpos >= 0) & (kpos < SAMPLE_LEN)

    def score_fn(qm, kv):
        sl = slice(kv * 128, (kv + 1) * 128)
        kloc = jnp.concatenate([k0[:, sl], k1[:, sl], k2[:, sl]], axis=0)
        s_loc = jnp.where(valid, _dot_nt(qm, kloc) * scale, NEG_INF)
        return [s_loc, _dot_nt(qm, kc_ref[:, sl]) * scale]

    def value_fn(ps, kv):
        sl = slice(kv * 128, (kv + 1) * 128)
        vloc = jnp.concatenate([v0[:, sl], v1[:, sl], v2[:, sl]], axis=0)
        return _dot(ps[0].astype(BF16), vloc) + _dot(ps[1].astype(BF16), vc_ref[:, sl])

    _sink_heads(q_ref, sink_ref, o_ref, score_fn, value_fn, scale)


def win_sample_call(sink, q, kpad, vpad, kc, vc):
    nq = SAMPLE_LEN // WIN
    qoff = T_P // WIN
    per = nq + 2
    loc = lambda d: (lambda b, qi: (b * per + qi + d, 0))
    return pl.pallas_call(
        functools.partial(_win_sample_kernel, scale=64 ** -0.5),
        grid=(N_SAMPLE_SEQ, nq),
        in_specs=[
            pl.BlockSpec(memory_space=pltpu.SMEM),
            pl.BlockSpec((WIN, 512), lambda b, qi: (qoff + b * nq + qi, 0)),
            pl.BlockSpec((WIN, 256), loc(0)),
            pl.BlockSpec((WIN, 256), loc(1)),
            pl.BlockSpec((WIN, 256), loc(2)),
            pl.BlockSpec((WIN, 256), loc(0)),
            pl.BlockSpec((WIN, 256), loc(1)),
            pl.BlockSpec((WIN, 256), loc(2)),
            pl.BlockSpec((PAST, 256), lambda b, qi: (b, 0)),
            pl.BlockSpec((PAST, 256), lambda b, qi: (b, 0)),
        ],
        out_specs=pl.BlockSpec((WIN, 512), lambda b, qi: (b * nq + qi, 0)),
        out_shape=jax.ShapeDtypeStruct((T_S, 512), BF16),
        compiler_params=_cparams(("arbitrary", "arbitrary")),
        name="win_sample",
    )(sink, q, kpad, kpad, kpad, vpad, vpad, vpad, kc, vc)


def _pool_kernel(xp_ref, xc_ref, xn_ref, w_ref, sc_ref, o_ref):
    i = pl.program_id(0)
    jj = (i - NBLK_P) % SBLK
    first = jnp.logical_or(i < NBLK_P, jj == 0)
    last = jnp.logical_or(i < NBLK_P, jj == SBLK - 1)
    ext = TM + 2 * POOL_HALO
    r = lax.broadcasted_iota(I32, (TM, ext), 0)
    c = lax.broadcasted_iota(I32, (TM, ext), 1) - POOL_HALO
    ok = jnp.logical_and(jnp.logical_or(c >= 0, jnp.logical_not(first)), jnp.logical_or(c < TM, jnp.logical_not(last)))
    rr = lax.broadcasted_iota(I32, (TM, 1), 0)
    parts = []
    for g, w in enumerate(POOL_WINDOWS):
        sl = slice(g * 128, (g + 1) * 128)
        x = xc_ref[:, sl]
        xe = jnp.concatenate([xp_ref[:, sl], x, xn_ref[:, sl]], axis=0)
        lo = r - w // 2
        band = (c >= lo) & (c < lo + w) & ok
        a = jnp.where(band, 1.0, 0.0).astype(BF16)
        lo1 = rr - w // 2
        lo_c = jnp.where(first, jnp.maximum(lo1, 0), lo1)
        hi_c = jnp.where(last, jnp.minimum(lo1 + w, TM), lo1 + w)
        cnt = (hi_c - lo_c).astype(F32)
        hi_x, lo_x = _split_bf16(xe)
        pooled = (_dot(a, hi_x) + _dot(a, lo_x)) / cnt
        y = _dot((pooled - x).astype(BF16), w_ref[g])
        parts.append(y)
    o_ref[...] = (jnp.concatenate(parts, axis=-1) * sc_ref[...]).astype(o_ref.dtype)


def pool_call(proj, w_pool, scale):
    hb = TM // POOL_HALO
    nh = T // POOL_HALO
    return pl.pallas_call(
        _pool_kernel,
        grid=(NBLK,),
        in_specs=[
            pl.BlockSpec((POOL_HALO, 512), lambda i: (jnp.maximum(i * hb - 1, 0), 0)),
            pl.BlockSpec((TM, 512), lambda i: (i, 0)),
            pl.BlockSpec((POOL_HALO, 512), lambda i: (jnp.minimum((i + 1) * hb, nh - 1), 0)),
            pl.BlockSpec((4, 128, 128), lambda i: (0, 0, 0)),
            pl.BlockSpec((1, 512), lambda i: (0, 0)),
        ],
        out_specs=pl.BlockSpec((TM, 512), lambda i: (i, 0)),
        out_shape=jax.ShapeDtypeStruct((T, 512), BF16),
        compiler_params=_cparams(("arbitrary",)),
        name="pool",
    )(proj, proj, proj, w_pool, scale)


TOPK = 16
TK_TM = 128


def _top16(s, key, aux=None):
    cols = s.shape[1]
    r16 = lax.broadcasted_iota(I32, (TOPK, cols), 0)
    vals = jnp.zeros((TOPK, cols), F32)
    outs = jnp.zeros((TOPK, cols), F32)
    for r in range(TOPK):
        m = jnp.max(s, axis=0, keepdims=True)
        p = jnp.min(jnp.where(s == m, key, np.float32(1e9)), axis=0, keepdims=True)
        hit = key == p
        o = p if aux is None else jnp.max(jnp.where(hit, aux, -1.0), axis=0, keepdims=True)
        vals = jnp.where(r16 == r, m, vals)
        outs = jnp.where(r16 == r, o, outs)
        s = jnp.where(hit, -jnp.inf, s)
    return vals, outs


def _pair_candidates(v1, i1, v2, i2):
    cols = v1.shape[1]
    n8 = lax.broadcasted_iota(I32, (8, cols), 0)
    n16 = lax.broadcasted_iota(I32, (16, cols), 0)
    f8, f16 = n8.astype(F32), n16.astype(F32)
    sc, pos, ex = [], [], []

    def add(score, position, expert, mask):
        sc.append(score if mask is None else jnp.where(mask, score, -jnp.inf))
        pos.append(position)
        ex.append(expert)

    add(v1[0:1] + v2, f16, i1[0:1] * 128.0 + i2, None)
    add(v1[1:2] + v2[0:8], 16.0 + f8, i1[1:2] * 128.0 + i2[0:8], None)
    add(v1[2:3] + v2[0:8], 32.0 + f8, i1[2:3] * 128.0 + i2[0:8], n8 < 5)
    add(v1[3:4] + v2[0:8], 48.0 + f8, i1[3:4] * 128.0 + i2[0:8], n8 < 4)
    add(v1 + v2[0:1], f16 * 16.0, i1 * 128.0 + i2[0:1], n16 >= 4)
    add(v1[0:8] + v2[1:2], f8 * 16.0 + 1.0, i1[0:8] * 128.0 + i2[1:2], n8 >= 4)
    add(v1[0:8] + v2[2:3], f8 * 16.0 + 2.0, i1[0:8] * 128.0 + i2[2:3], n8 == 4)
    return jnp.concatenate(sc, axis=0), jnp.concatenate(pos, axis=0), jnp.concatenate(ex, axis=0)


def _peer_topk_kernel(q_ref, sk_ref, idx_ref, gate_ref):
    rowf = lax.broadcasted_iota(I32, (128, q_ref.shape[0]), 0).astype(F32)
    idx_rows, gate_rows = [], []
    for h in range(8):
        s1 = _dot_nt(sk_ref[2 * h], q_ref[:, (2 * h) * 128:(2 * h + 1) * 128])
        s2 = _dot_nt(sk_ref[2 * h + 1], q_ref[:, (2 * h + 1) * 128:(2 * h + 2) * 128])
        v1, i1 = _top16(s1, rowf)
        v2, i2 = _top16(s2, rowf)
        cand, pos, expert = _pair_candidates(v1, i1, v2, i2)
        tv, te = _top16(cand, pos, expert)
        e = jnp.exp(tv - tv[0:1])
        gate_rows.append(e / jnp.sum(e, axis=0, keepdims=True))
        idx_rows.append(te)
    idx_ref[...] = (jnp.concatenate(idx_rows, axis=0).T * 4.0).astype(I32)
    gate_ref[...] = jnp.concatenate(gate_rows, axis=0).T


def peer_topk_call(q, sk):
    row = lambda i: (i, 0)
    return pl.pallas_call(
        _peer_topk_kernel,
        grid=(T // TK_TM,),
        in_specs=[pl.BlockSpec((TK_TM, 2048), row), pl.BlockSpec((16, 128, 128), lambda i: (0, 0, 0))],
        out_specs=[pl.BlockSpec((TK_TM, NPICK), row), pl.BlockSpec((TK_TM, NPICK), row)],
        out_shape=[jax.ShapeDtypeStruct((T, NPICK), I32), jax.ShapeDtypeStruct((T, NPICK), F32)],
        compiler_params=_cparams(("arbitrary",)),
        name="peer_topk",
    )(q, sk)


def _gather_token(idx_ref, tab_ref, stage_ref, g, j):
    for k in range(NPICK):
        e4 = pl.multiple_of(idx_ref[g * PEER_TG + j, k], 4)
        stage_ref[pl.ds(j * 512 + 4 * k, 4), :] = tab_ref[pl.ds(e4, 4), :]


def _gather_schedule(idx_ref, idx_next_ref, tab_ref, stages, begin, token, end):
    ngroup = PEER_TB // PEER_TG

    @pl.when(pl.program_id(0) == 0)
    def _():
        for j in range(PEER_TG):
            _gather_token(idx_ref, tab_ref, stages[0], 0, j)

    for g in range(ngroup):
        base = g * PEER_TG
        state = begin(base)
        for j in range(PEER_TG):
            state = token(state, stages[g % NSTAGE], base, j)
            if g + 1 < ngroup:
                _gather_token(idx_ref, tab_ref, stages[(g + 1) % NSTAGE], g + 1, j)
            else:
                _gather_token(idx_next_ref, tab_ref, stages[ngroup % NSTAGE], 0, j)
        end(state, base)


def _staged_tiles(stage_ref, j):
    return pltpu.bitcast(stage_ref[pl.ds(j * 512, 512), :], BF16)


def _peer_u_kernel(idx_ref, idx_next_ref, x_ref, gate_ref, tab_ref, act_ref, *stages):
    col = lax.broadcasted_iota(I32, (16, 1024), 1)
    row = lax.broadcasted_iota(I32, (16, 1024), 0)
    diag = (col % 8) == (row % 8)
    gi = lax.broadcasted_iota(I32, (1024, 128), 0)
    gj = lax.broadcasted_iota(I32, (1024, 128), 1)
    gsum = jnp.where(gi // 8 == gj, 1.0, 0.0).astype(BF16)
    r8 = lax.broadcasted_iota(I32, (8, 1024), 0)

    def begin(base):
        return jnp.zeros((8, 1024), F32)

    def token(zs, stage_ref, base, j):
        xh, xl = _split_bf16(x_ref[base + j])
        lhs = jnp.concatenate([xh, xl], axis=0)
        z = _dot_nt(lhs, _staged_tiles(stage_ref, j))
        zr = jnp.sum(jnp.where(diag, z, 0.0), axis=0, keepdims=True)
        return jnp.where(r8 == j, zr, zs)

    def end(zs, base):
        zh, zl = _split_bf16(zs)
        sc = _dot(zh, gsum) + _dot(zl, gsum)
        gate = gate_ref[pl.ds(base, PEER_TG), :]
        act = 0.5 * sc * (1.0 + lax.erf(sc * np.float32(1.0 / np.sqrt(2.0)))) * gate
        act_ref[pl.ds(base, PEER_TG), :] = act

    _gather_schedule(idx_ref, idx_next_ref, tab_ref, stages, begin, token, end)


def _peer_v_kernel(idx_ref, idx_next_ref, act_ref, tab_ref, out_ref, *stages):
    ei = lax.broadcasted_iota(I32, (128, 1024), 0)
    ej = lax.broadcasted_iota(I32, (128, 1024), 1)
    expand = jnp.where(ej // 8 == ei, 1.0, 0.0).astype(BF16)
    col = lax.broadcasted_iota(I32, (8, 1024), 1)
    row = lax.broadcasted_iota(I32, (8, 1024), 0)
    diag = (col % 8) == row

    def begin(base):
        ah, al = _split_bf16(act_ref[pl.ds(base, PEER_TG), :])
        return _dot(ah, expand), _dot(al, expand)

    def token(rep, stage_ref, base, j):
        lh = jnp.where(diag, rep[0][j:j + 1, :], 0.0).astype(BF16)
        ll = jnp.where(diag, rep[1][j:j + 1, :], 0.0).astype(BF16)
        o = _dot(jnp.concatenate([lh, ll], axis=0), _staged_tiles(stage_ref, j))
        out_ref[base + j] = o[0:8] + o[8:16]
        return rep

    def end(rep, base):
        pass

    _gather_schedule(idx_ref, idx_next_ref, tab_ref, stages, begin, token, end)


def _stage_scratch():
    return [pltpu.VMEM((PEER_TG * 512, 128), jnp.uint32) for _ in range(NSTAGE)]


def _idx_specs():
    per = PEER_TB // PEER_TG
    last = T // PEER_TG - 1
    return [
        pl.BlockSpec((PEER_TB, NPICK), lambda i: (i, 0), memory_space=pltpu.SMEM),
        pl.BlockSpec((PEER_TG, NPICK), lambda i: (jnp.minimum((i + 1) * per, last), 0), memory_space=pltpu.SMEM),
    ]


def _pack_table(tab):
    t = lax.bitcast_convert_type(tab.astype(BF16), jnp.uint16).reshape(P_EXPERTS, 4, 2, 128).astype(jnp.uint32)
    return (t[:, :, 0, :] | (t[:, :, 1, :] << 16)).reshape(P_EXPERTS * 4, 128)


def peer_u_call(idx, x3, gate, tab):
    tb = PEER_TB
    return pl.pallas_call(
        _peer_u_kernel,
        grid=(T // tb,),
        in_specs=_idx_specs() + [
            pl.BlockSpec((tb, 8, 128), lambda i: (i, 0, 0)),
            pl.BlockSpec((tb, NPICK), lambda i: (i, 0)),
            pl.BlockSpec((P_EXPERTS * 4, 128), lambda i: (0, 0), pipeline_mode=pl.Buffered(1)),
        ],
        out_specs=pl.BlockSpec((tb, NPICK), lambda i: (i, 0)),
        out_shape=jax.ShapeDtypeStruct((T, NPICK), F32),
        scratch_shapes=_stage_scratch(),
        compiler_params=_cparams(("arbitrary",), VMEM_LIMIT),
        name="peer_u",
    )(idx, idx, x3, gate, tab)


def peer_v_call(idx, act, tab):
    tb = PEER_TB
    return pl.pallas_call(
        _peer_v_kernel,
        grid=(T // tb,),
        in_specs=_idx_specs() + [
            pl.BlockSpec((tb, NPICK), lambda i: (i, 0)),
            pl.BlockSpec((P_EXPERTS * 4, 128), lambda i: (0, 0), pipeline_mode=pl.Buffered(1)),
        ],
        out_specs=pl.BlockSpec((tb, 8, 128), lambda i: (i, 0, 0)),
        out_shape=jax.ShapeDtypeStruct((T, 8, 128), F32),
        scratch_shapes=_stage_scratch(),
        compiler_params=_cparams(("arbitrary",), VMEM_LIMIT),
        name="peer_v",
    )(idx, idx, act, tab)


def _even_in_weight(w):
    qa, ka, va, ra, glf, glb, cq, ckv, kr = jnp.split(w, np.cumsum([256, 256, 512, 512, 16, 16, 256, 256, 64])[:-1].tolist(), axis=-1)
    pad = jnp.zeros((D, 32), w.dtype)
    return jnp.concatenate([qa, ka, va, ra, cq, ckv, glf, glb, pad, kr], axis=-1).astype(BF16)


def _gate_up_pad(w_gu):
    z = jnp.zeros((2, 128, 256), F32)
    z = z.at[0, 0:16].set(w_gu[0])
    z = z.at[1, 16:32].set(w_gu[1])
    return z.astype(BF16)


def _uk_pad(w_uk):
    w = w_uk.reshape(256, 8, 64)
    return jnp.concatenate([w, jnp.zeros_like(w)], axis=-1).reshape(256, 1024).astype(BF16)


def _dup_heads(x):
    a, b = x[:, :64], x[:, 64:]
    return jnp.concatenate([a, a, b, b], axis=-1).astype(BF16)


def _pad_latent(x):
    x = x.reshape(N_SAMPLE_SEQ, SAMPLE_LEN, 256)
    x = jnp.pad(x, ((0, 0), (WIN, WIN), (0, 0)))
    return x.reshape(N_SAMPLE_SEQ * (SAMPLE_LEN + 2 * WIN), 256)


def kernel(x_prompt, x_sample, state_gla, cache_mla_ckv, cache_mla_krope, cache_win_kv, c, c_ctx, g_norm, w_ada, b_ada, w_in_even, w_gate_up, b_gate_up, g_gla_out, g_mla_cq, g_mla_ckv, w_mla_uq, w_mla_uk, w_mla_uv, g_mla_qn, g_mla_kn, w_out_even, w_in_odd, w_pool, pool_scale, g_win_qn, g_win_kn, win_sink, w_out_odd, peer_wq, peer_subkeys, peer_u, peer_v):
    depth = w_ada.shape[0]
    x = jnp.concatenate([x_prompt.reshape(T_P, D), x_sample.reshape(T_S, D)], axis=0)
    cond8 = jnp.concatenate([c_ctx[None], c, jnp.zeros((3, D), F32)], axis=0)
    mods_all = ada_call(cond8, w_ada, b_ada).reshape(depth, 8, 6, D)
    tables_b = _rope_tables("upper")
    tables_d = _rope_tables("all")

    gla_states, mla_ckv, mla_kr, win_kv = [], [], [], []
    for l in range(depth):
        mods = mods_all[l]
        if l % 2 == 0:
            e = l // 2
            (proj,) = modproj_call(x, g_norm[l, 0], mods, _even_in_weight(w_in_even[e]), 0, F32, False)
            s0 = jnp.concatenate([jnp.zeros((N_PROMPT_SEQ, 2, 4, 64, 128), F32), state_gla[:, e]], axis=0)
            s0t = jnp.swapaxes(s0.reshape(-1, 2, 256, 128), -1, -2)
            o_f, o_b, sfin = gla_call(proj, _gate_up_pad(w_gate_up[e]), b_gate_up[e].reshape(2, 1, 256), s0t)
            sfin = jnp.swapaxes(sfin[:N_PROMPT_SEQ], -1, -2).reshape(N_PROMPT_SEQ, 2, 4, 64, 128)
            gla_states.append(sfin)
            wuk_pad = _uk_pad(w_mla_uk[e])
            wuv = w_mla_uv[e].astype(BF16)
            gkn = g_mla_kn[e].reshape(1, 128)
            qb, kb, vb, ckv_n, krf = mla_prep_call(
                proj, g_mla_cq[e].reshape(1, 256), g_mla_ckv[e].reshape(1, 256), w_mla_uq[e].astype(BF16),
                wuk_pad, wuv, g_mla_qn[e].reshape(1, 128), gkn, tables_b)
            mla_ckv.append(ckv_n[:T_P].reshape(N_PROMPT_SEQ, PROMPT_LEN, 256))
            mla_kr.append(krf[:T_P, 64:].reshape(N_PROMPT_SEQ, PROMPT_LEN, 64))
            kr_ctx = cache_mla_krope[:, e].reshape(N_SAMPLE_SEQ * PAST, 64)
            kr_pad = jnp.concatenate([jnp.zeros_like(kr_ctx), kr_ctx], axis=-1)
            kc, vc = mla_ctx_call(cache_mla_ckv[:, e].reshape(N_SAMPLE_SEQ * PAST, 256), kr_pad, wuk_pad, wuv, gkn)
            ob = jnp.concatenate([mla_attn_prompt_call(qb, kb, vb), mla_attn_sample_call(qb, kb, vb, kc, vc)], axis=0)
            x = outproj_call(x, o_f, o_b, proj, 2, g_gla_out[e].reshape(1, 128), ob, w_out_even[e].astype(BF16), mods, True)
        else:
            o = l // 2
            (proj,) = modproj_call(x, g_norm[l, 0], mods, w_in_odd[o].astype(BF16), 0, F32, False)
            oc = pool_call(proj, w_pool[o].astype(BF16), pool_scale[o].reshape(1, 512))
            gq2 = jnp.tile(g_win_qn[o], 2).reshape(1, 128)
            gk2 = jnp.tile(g_win_kn[o], 2).reshape(1, 128)
            qw, kd, vd, kn, vn = odd_prep_call(proj, gq2, gk2, tables_d)
            k_p = kn[:T_P].reshape(N_PROMPT_SEQ, PROMPT_LEN, 2, 64)
            v_p = vn[:T_P].reshape(N_PROMPT_SEQ, PROMPT_LEN, 2, 64)
            win_kv.append(jnp.stack([k_p, v_p], axis=1))
            kc = _dup_heads(cache_win_kv[:, o, 0].reshape(N_SAMPLE_SEQ * PAST, 128))
            vc = _dup_heads(cache_win_kv[:, o, 1].reshape(N_SAMPLE_SEQ * PAST, 128))
            od = jnp.concatenate([
                win_prompt_call(win_sink[o], qw, kd, vd),
                win_sample_call(win_sink[o], qw, _pad_latent(kd[T_P:]), _pad_latent(vd[T_P:]), kc, vc)], axis=0)
            x = outproj_call(x, oc, oc, proj, 0, jnp.ones((1, 128), F32), od, w_out_odd[o].astype(BF16), mods, False)
        q, h = modproj_call(x, g_norm[l, 1], mods, peer_wq[l].astype(BF16), 3, BF16, True)
        idx, gate = peer_topk_call(q, peer_subkeys[l].reshape(16, 128, 128).astype(BF16))
        tab_u = _pack_table(peer_u[l])
        tab_v = _pack_table(peer_v[l])
        act = peer_u_call(idx, h.reshape(T, 8, 128), gate, tab_u)
        y = peer_v_call(idx, act, tab_v).reshape(T, D)
        x = resid_call(x, y, mods, 5)

    return (
        x[:T_P].reshape(N_PROMPT_SEQ, PROMPT_LEN, D),
        x[T_P:].reshape(N_SAMPLE_SEQ, SAMPLE_LEN, D),
        jnp.stack(gla_states, axis=1),
        jnp.stack(mla_ckv, axis=1),
        jnp.stack(mla_kr, axis=1),
        jnp.stack(win_kv, axis=1),
    )
```

```python
import functools

import jax
import jax.numpy as jnp
import numpy as np
from jax import lax
from jax.experimental import pallas as pl
from jax.experimental.pallas import tpu as pltpu

F32 = jnp.float32
BF16 = jnp.bfloat16
I32 = jnp.int32

D = 1024
N_PROMPT_SEQ = 32
PROMPT_LEN = 256
N_SAMPLE_SEQ = 4
SAMPLE_LEN = 4096
PAST = 512
T_P = N_PROMPT_SEQ * PROMPT_LEN
T_S = N_SAMPLE_SEQ * SAMPLE_LEN
T = T_P + T_S
GRID_W = 64
EPS = 1e-6
NEG_INF = -1e30
ROPE_BASE = 10000.0

TM = 256
NBLK = T // TM
NBLK_P = T_P // TM
SBLK = SAMPLE_LEN // TM

EVEN_W = 2176
ODD_W = 1280
POOL_WINDOWS = (2, 4, 8, 16)
POOL_HALO = 8

P_EXPERTS = 16384
NPICK = 128
PEER_TB = 64
PEER_TG = 8
VMEM_LIMIT = 56 * 1024 * 1024


def _cparams(sem, vmem=None):
    return pltpu.CompilerParams(dimension_semantics=sem, vmem_limit_bytes=vmem)


def _mod_group(i):
    return jnp.where(i < NBLK_P, 0, 1 + (i - NBLK_P) // SBLK)


def _split_bf16(x):
    hi = x.astype(BF16)
    lo = (x - hi.astype(F32)).astype(BF16)
    return hi, lo


def _dot(a, b):
    return jnp.dot(a, b, preferred_element_type=F32)


def _dot_nt(a, b):
    return lax.dot_general(a, b, (((1,), (1,)), ((), ())), preferred_element_type=F32)


def _rms(x, g):
    ms = jnp.mean(x * x, axis=-1, keepdims=True)
    return x * lax.rsqrt(ms + EPS) * g


def _ada_kernel(c_ref, w_ref, b_ref, o_ref):
    c = c_ref[...]
    s = c / (1.0 + jnp.exp(-c))
    o_ref[0] = _dot(s.astype(BF16), w_ref[0].astype(BF16)) + b_ref[0]


def ada_call(cond8, w_ada, b_ada):
    depth, _, n6 = w_ada.shape
    tn = 1536
    return pl.pallas_call(
        _ada_kernel,
        grid=(depth, n6 // tn),
        in_specs=[
            pl.BlockSpec((8, D), lambda l, j: (0, 0)),
            pl.BlockSpec((1, D, tn), lambda l, j: (l, 0, j)),
            pl.BlockSpec((1, 1, tn), lambda l, j: (l, 0, j)),
        ],
        out_specs=pl.BlockSpec((1, 8, tn), lambda l, j: (l, 0, j)),
        out_shape=jax.ShapeDtypeStruct((depth, 8, n6), F32),
        compiler_params=_cparams(("arbitrary", "arbitrary"), 40 * 1024 * 1024),
        name="ada",
    )(cond8, w_ada, b_ada.reshape(depth, 1, n6))


def _modproj_kernel(x_ref, g_ref, mod_ref, w_ref, o_ref, *h_ref, a):
    x = x_ref[...]
    m = mod_ref[0]
    h = _rms(x, g_ref[...]) * (1.0 + m[a + 1:a + 2]) + m[a:a + 1]
    o_ref[...] = _dot(h.astype(BF16), w_ref[...]).astype(o_ref.dtype)
    if h_ref:
        h_ref[0][...] = h


def modproj_call(x, g, mods, w, a, out_dtype, with_h):
    n = w.shape[1]
    out_shape = [jax.ShapeDtypeStruct((T, n), out_dtype)]
    out_specs = [pl.BlockSpec((TM, n), lambda i: (i, 0))]
    if with_h:
        out_shape.append(jax.ShapeDtypeStruct((T, D), F32))
        out_specs.append(pl.BlockSpec((TM, D), lambda i: (i, 0)))
    return pl.pallas_call(
        functools.partial(_modproj_kernel, a=a),
        grid=(NBLK,),
        in_specs=[
            pl.BlockSpec((TM, D), lambda i: (i, 0)),
            pl.BlockSpec((1, D), lambda i: (0, 0)),
            pl.BlockSpec((1, 6, D), lambda i: (_mod_group(i), 0, 0)),
            pl.BlockSpec((D, n), lambda i: (0, 0)),
        ],
        out_specs=out_specs,
        out_shape=out_shape,
        compiler_params=_cparams(("arbitrary",), 40 * 1024 * 1024),
        name="modproj",
    )(x, g.reshape(1, D), mods, w)


def _resid_kernel(x_ref, y_ref, mod_ref, o_ref, *, a):
    o_ref[...] = x_ref[...] + mod_ref[0][a:a + 1] * y_ref[...]


def resid_call(x, y, mods, a):
    return pl.pallas_call(
        functools.partial(_resid_kernel, a=a),
        grid=(NBLK,),
        in_specs=[
            pl.BlockSpec((TM, D), lambda i: (i, 0)),
            pl.BlockSpec((TM, D), lambda i: (i, 0)),
            pl.BlockSpec((1, 6, D), lambda i: (_mod_group(i), 0, 0)),
        ],
        out_specs=pl.BlockSpec((TM, D), lambda i: (i, 0)),
        out_shape=jax.ShapeDtypeStruct((T, D), F32),
        compiler_params=_cparams(("arbitrary",)),
        name="resid",
    )(x, y, mods)


def _outproj_kernel(x_ref, a_ref, a2_ref, r_ref, ggo_ref, b_ref, w_ref, mod_ref, o_ref, *, gla_post):
    if gla_post:
        o = a_ref[...] + a2_ref[...]
        r = r_ref[...]
        parts = []
        for h in range(4):
            seg = _rms(o[:, h * 128:(h + 1) * 128], ggo_ref[...])
            rr = r[:, h * 128:(h + 1) * 128]
            parts.append(seg * (rr / (1.0 + jnp.exp(-rr))))
        mix_a = jnp.concatenate(parts, axis=-1)
    else:
        mix_a = a_ref[...]
    w = w_ref[...]
    y = _dot(mix_a.astype(BF16), w[0:512]) + _dot(b_ref[...].astype(BF16), w[512:1024])
    o_ref[...] = x_ref[...] + mod_ref[0][2:3] * y


def outproj_call(x, mix_a, mix_a2, proj, ra_col, g_go, mix_b, w, mods, gla_post):
    row = lambda i: (i, 0)
    return pl.pallas_call(
        functools.partial(_outproj_kernel, gla_post=gla_post),
        grid=(NBLK,),
        in_specs=[
            pl.BlockSpec((TM, D), row),
            pl.BlockSpec((TM, 512), row),
            pl.BlockSpec((TM, 512), row),
            pl.BlockSpec((TM, 512), lambda i: (i, ra_col)),
            pl.BlockSpec((1, 128), lambda i: (0, 0)),
            pl.BlockSpec((TM, 512), row),
            pl.BlockSpec((D, D), lambda i: (0, 0)),
            pl.BlockSpec((1, 6, D), lambda i: (_mod_group(i), 0, 0)),
        ],
        out_specs=pl.BlockSpec((TM, D), row),
        out_shape=jax.ShapeDtypeStruct((T, D), F32),
        compiler_params=_cparams(("arbitrary",), 40 * 1024 * 1024),
        name="outproj",
    )(x, mix_a, mix_a2, proj, g_go, mix_b, w, mods)


GLA_CHUNK = 64


def _log_sigmoid(z):
    return jnp.minimum(z, 0.0) - jnp.log1p(jnp.exp(-jnp.abs(z)))


def _gla_direction(qk_ref, v_ref, gl_ref, wgu, bgu, st_ref, o_ref, reverse):
    ri = lax.broadcasted_iota(I32, (GLA_CHUNK, GLA_CHUNK), 0)
    ci = lax.broadcasted_iota(I32, (GLA_CHUNK, GLA_CHUNK), 1)
    keep = (ci >= ri) if reverse else (ci <= ri)
    tri = jnp.where(keep, 1.0, 0.0).astype(BF16)
    lane = lax.broadcasted_iota(I32, (GLA_CHUNK, 256), 1)
    lane_s = lax.broadcasted_iota(I32, (128, 256), 1)
    nchunk = TM // GLA_CHUNK
    order = range(nchunk - 1, -1, -1) if reverse else range(nchunk)
    for c in order:
        r0 = c * GLA_CHUNK
        q = qk_ref[r0:r0 + GLA_CHUNK, 0:256] * 0.125
        k = qk_ref[r0:r0 + GLA_CHUNK, 256:512]
        v = v_ref[r0:r0 + GLA_CHUNK, :]
        z = _dot(gl_ref[r0:r0 + GLA_CHUNK, :].astype(BF16), wgu) + bgu
        la = _log_sigmoid(z) * (1.0 / 16.0)
        la_hi, la_lo = _split_bf16(la)
        b = _dot(tri, la_hi) + _dot(tri, la_lo)
        b_tot = b[0:1, :] if reverse else b[GLA_CHUNK - 1:GLA_CHUNK, :]
        q_dec = q * jnp.exp(b)
        k_intra = (k * jnp.exp(-b)).astype(BF16)
        k_state = (k * jnp.exp(b_tot - b)).astype(BF16)
        st = st_ref[...]
        st_b = st.astype(BF16)
        v_b = v.astype(BF16)
        vt = v.T.astype(BF16)
        outs = []
        upd = jnp.zeros((128, 256), F32)
        for h in range(4):
            qm = jnp.where(lane // 64 == h, q_dec, 0.0).astype(BF16)
            att = jnp.where(keep, _dot_nt(qm, k_intra), 0.0)
            o_h = _dot(att.astype(BF16), v_b[:, h * 128:(h + 1) * 128]) + _dot_nt(qm, st_b)
            outs.append(o_h)
            u_h = _dot(vt[h * 128:(h + 1) * 128, :], k_state)
            upd = jnp.where(lane_s // 64 == h, u_h, upd)
        st_ref[...] = st * jnp.exp(b_tot) + upd
        o_ref[r0:r0 + GLA_CHUNK, :] = jnp.concatenate(outs, axis=-1)


def _gla_kernel(qkf, vf, glf, qkb, vb, glb, wgu_ref, bgu_ref, s0_ref, of_ref, ob_ref, sfin_ref, sf_scr, sb_scr):
    i = pl.program_id(0)
    jj = (i - NBLK_P) % SBLK
    first = jnp.logical_or(i < NBLK_P, jj == 0)
    last = jnp.logical_or(i < NBLK_P, jj == SBLK - 1)

    @pl.when(first)
    def _():
        sf_scr[...] = s0_ref[0, 0]
        sb_scr[...] = s0_ref[0, 1]

    _gla_direction(qkf, vf, glf, wgu_ref[0], bgu_ref[0], sf_scr, of_ref, False)
    _gla_direction(qkb, vb, glb, wgu_ref[1], bgu_ref[1], sb_scr, ob_ref, True)

    @pl.when(last)
    def _():
        sfin_ref[0, 0] = sf_scr[...]
        sfin_ref[0, 1] = sb_scr[...]


def _seq_of_block(i):
    return jnp.where(i < NBLK_P, i, NBLK_P + (i - NBLK_P) // SBLK)


def _rev_block(i):
    s = (i - NBLK_P) // SBLK
    jj = (i - NBLK_P) % SBLK
    return jnp.where(i < NBLK_P, i, NBLK_P + s * SBLK + (SBLK - 1 - jj))


def gla_call(proj, wgu_pad, bgu, s0t):
    nseq = N_PROMPT_SEQ + N_SAMPLE_SEQ
    fwd = lambda c: (lambda i: (i, c))
    bwd = lambda c: (lambda i: (_rev_block(i), c))
    return pl.pallas_call(
        _gla_kernel,
        grid=(NBLK,),
        in_specs=[
            pl.BlockSpec((TM, 512), fwd(0)),
            pl.BlockSpec((TM, 512), fwd(1)),
            pl.BlockSpec((TM, 128), fwd(16)),
            pl.BlockSpec((TM, 512), bwd(0)),
            pl.BlockSpec((TM, 512), bwd(1)),
            pl.BlockSpec((TM, 128), bwd(16)),
            pl.BlockSpec((2, 128, 256), lambda i: (0, 0, 0)),
            pl.BlockSpec((2, 1, 256), lambda i: (0, 0, 0)),
            pl.BlockSpec((1, 2, 128, 256), lambda i: (_seq_of_block(i), 0, 0, 0)),
        ],
        out_specs=[
            pl.BlockSpec((TM, 512), fwd(0)),
            pl.BlockSpec((TM, 512), bwd(0)),
            pl.BlockSpec((1, 2, 128, 256), lambda i: (_seq_of_block(i), 0, 0, 0)),
        ],
        out_shape=[
            jax.ShapeDtypeStruct((T, 512), F32),
            jax.ShapeDtypeStruct((T, 512), F32),
            jax.ShapeDtypeStruct((nseq, 2, 128, 256), F32),
        ],
        scratch_shapes=[pltpu.VMEM((128, 256), F32), pltpu.VMEM((128, 256), F32)],
        compiler_params=_cparams(("arbitrary",), 40 * 1024 * 1024),
        name="gla",
    )(proj, proj, proj, proj, proj, proj, wgu_pad, bgu, s0t)


def _rope(x, c, sa, sb):
    return x * c + pltpu.roll(x, 112, 1) * sa + pltpu.roll(x, 16, 1) * sb


def _rope_tables(rot_lanes):
    rows = SAMPLE_LEN // GRID_W
    quarter = 16
    inv = ROPE_BASE ** (-jnp.arange(quarter, dtype=F32) / quarter)
    row = jnp.repeat(jnp.arange(rows, dtype=F32), GRID_W)
    col = jnp.tile(jnp.arange(GRID_W, dtype=F32), rows)
    ar = row[:, None] * inv
    ac = col[:, None] * inv
    ang = jnp.concatenate([ar, ar, ac, ac], axis=-1)
    cos, sin = jnp.cos(ang), jnp.sin(ang)
    seg = (np.arange(64) // 16) % 2
    sa64 = jnp.where(seg == 0, -sin, 0.0)
    sb64 = jnp.where(seg == 1, sin, 0.0)
    if rot_lanes == "upper":
        c = jnp.concatenate([jnp.ones_like(cos), cos], axis=-1)
        sa = jnp.concatenate([jnp.zeros_like(sin), sa64], axis=-1)
        sb = jnp.concatenate([jnp.zeros_like(sin), sb64], axis=-1)
    else:
        c = jnp.concatenate([cos, cos], axis=-1)
        sa = jnp.concatenate([sa64, sa64], axis=-1)
        sb = jnp.concatenate([sb64, sb64], axis=-1)

    def full(t, fill):
        t = jnp.tile(t, (N_SAMPLE_SEQ, 1))
        return jnp.concatenate([jnp.full((T_P, 128), fill, F32), t], axis=0)

    return full(c, 1.0), full(sa, 0.0), full(sb, 0.0)


def _mla_prep_kernel(*refs, with_q, norm_ckv, rope):
    it = iter(refs)
    cq_ref = next(it) if with_q else None
    ckv_ref = next(it)
    kr_ref = next(it)
    gcq_ref = next(it) if with_q else None
    gckv_ref = next(it) if norm_ckv else None
    wuq_ref = next(it) if with_q else None
    wuk_ref = next(it)
    wuv_ref = next(it)
    gqn_ref = next(it) if with_q else None
    gkn_ref = next(it)
    if rope:
        c_ref, sa_ref, sb_ref = next(it), next(it), next(it)
    q_out = next(it) if with_q else None
    k_out = next(it)
    v_out = next(it)
    ckv_out = next(it) if norm_ckv else None
    kr_out = next(it) if norm_ckv else None

    if rope:
        ct, sat, sbt = c_ref[...], sa_ref[...], sb_ref[...]
    ckv = ckv_ref[...]
    if norm_ckv:
        ckv = _rms(ckv, gckv_ref[...])
        ckv_out[...] = ckv
    lane = lax.broadcasted_iota(I32, kr_ref.shape, 1)
    krf = jnp.where(lane >= 64, kr_ref[...], 0.0)
    if norm_ckv:
        kr_out[...] = krf
    ckv_b = ckv.astype(BF16)
    knope = _dot(ckv_b, wuk_ref[...])
    v_out[...] = _dot(ckv_b, wuv_ref[...]).astype(v_out.dtype)
    if with_q:
        qb = _dot(_rms(cq_ref[...], gcq_ref[...]).astype(BF16), wuq_ref[...])
    for h in range(8):
        sl = slice(h * 128, (h + 1) * 128)
        kh = _rms(knope[:, sl] + krf, gkn_ref[...])
        if rope:
            kh = _rope(kh, ct, sat, sbt)
        k_out[:, sl] = kh.astype(k_out.dtype)
        if with_q:
            qh = _rms(qb[:, sl], gqn_ref[...])
            if rope:
                qh = _rope(qh, ct, sat, sbt)
            q_out[:, sl] = qh.astype(q_out.dtype)


def mla_prep_call(proj, gcq, gckv, wuq, wuk_pad, wuv, gqn, gkn, tables):
    row = lambda i: (i, 0)
    const = lambda i: (0, 0)
    return pl.pallas_call(
        functools.partial(_mla_prep_kernel, with_q=True, norm_ckv=True, rope=True),
        grid=(NBLK,),
        in_specs=[
            pl.BlockSpec((TM, 256), lambda i: (i, 6)),
            pl.BlockSpec((TM, 256), lambda i: (i, 7)),
            pl.BlockSpec((TM, 128), lambda i: (i, 16)),
            pl.BlockSpec((1, 256), const),
            pl.BlockSpec((1, 256), const),
            pl.BlockSpec((256, 1024), const),
            pl.BlockSpec((256, 1024), const),
            pl.BlockSpec((256, 512), const),
            pl.BlockSpec((1, 128), const),
            pl.BlockSpec((1, 128), const),
            pl.BlockSpec((TM, 128), row),
            pl.BlockSpec((TM, 128), row),
            pl.BlockSpec((TM, 128), row),
        ],
        out_specs=[
            pl.BlockSpec((TM, 1024), row),
            pl.BlockSpec((TM, 1024), row),
            pl.BlockSpec((TM, 512), row),
            pl.BlockSpec((TM, 256), row),
            pl.BlockSpec((TM, 128), row),
        ],
        out_shape=[
            jax.ShapeDtypeStruct((T, 1024), BF16),
            jax.ShapeDtypeStruct((T, 1024), BF16),
            jax.ShapeDtypeStruct((T, 512), BF16),
            jax.ShapeDtypeStruct((T, 256), F32),
            jax.ShapeDtypeStruct((T, 128), F32),
        ],
        compiler_params=_cparams(("arbitrary",), 40 * 1024 * 1024),
        name="mla_prep",
    )(proj, proj, proj, gcq, gckv, wuq, wuk_pad, wuv, gqn, gkn, *tables)


def mla_ctx_call(ckv_ctx, kr_pad, wuk_pad, wuv, gkn):
    n = ckv_ctx.shape[0]
    row = lambda i: (i, 0)
    const = lambda i: (0, 0)
    return pl.pallas_call(
        functools.partial(_mla_prep_kernel, with_q=False, norm_ckv=False, rope=False),
        grid=(n // TM,),
        in_specs=[
            pl.BlockSpec((TM, 256), row),
            pl.BlockSpec((TM, 128), row),
            pl.BlockSpec((256, 1024), const),
            pl.BlockSpec((256, 512), const),
            pl.BlockSpec((1, 128), const),
        ],
        out_specs=[pl.BlockSpec((TM, 1024), row), pl.BlockSpec((TM, 512), row)],
        out_shape=[jax.ShapeDtypeStruct((n, 1024), BF16), jax.ShapeDtypeStruct((n, 512), BF16)],
        compiler_params=_cparams(("arbitrary",), 40 * 1024 * 1024),
        name="mla_ctx",
    )(ckv_ctx, kr_pad, wuk_pad, wuv, gkn)


def _attn_dense_kernel(q_ref, k_ref, v_ref, *rest, scale, has_ctx):
    if has_ctx:
        kc_ref, vc_ref, o_ref = rest
    else:
        (o_ref,) = rest
    v = v_ref[...]
    outs = []
    for hh in range(2):
        sl = slice(hh * 128, (hh + 1) * 128)
        q = q_ref[:, sl]
        s = _dot_nt(q, k_ref[:, sl]) * scale
        m = jnp.max(s, axis=-1, keepdims=True)
        if has_ctx:
            sc = _dot_nt(q, kc_ref[:, sl]) * scale
            m = jnp.maximum(m, jnp.max(sc, axis=-1, keepdims=True))
        p = jnp.exp(s - m)
        l = jnp.sum(p, axis=-1, keepdims=True)
        o = _dot(p.astype(BF16), v)
        if has_ctx:
            pc = jnp.exp(sc - m)
            l = l + jnp.sum(pc, axis=-1, keepdims=True)
            o = o + _dot(pc.astype(BF16), vc_ref[...])
        outs.append(o / l)
    lane = lax.broadcasted_iota(I32, outs[0].shape, 1)
    o_ref[...] = jnp.where(lane < 64, outs[0], outs[1]).astype(o_ref.dtype)


def mla_attn_prompt_call(qb, kb, vb):
    blk = lambda b, hp: (b, hp)
    return pl.pallas_call(
        functools.partial(_attn_dense_kernel, scale=128 ** -0.5, has_ctx=False),
        grid=(N_PROMPT_SEQ, 4),
        in_specs=[
            pl.BlockSpec((PROMPT_LEN, 256), blk),
            pl.BlockSpec((PROMPT_LEN, 256), blk),
            pl.BlockSpec((PROMPT_LEN, 128), blk),
        ],
        out_specs=pl.BlockSpec((PROMPT_LEN, 128), blk),
        out_shape=jax.ShapeDtypeStruct((T_P, 512), BF16),
        compiler_params=_cparams(("arbitrary", "arbitrary")),
        name="mla_attn_prompt",
    )(qb, kb, vb)


def mla_attn_sample_call(qb, kb, vb, kc, vc):
    tq = 256
    nq = SAMPLE_LEN // tq
    qoff = T_P // tq
    soff = T_P // SAMPLE_LEN
    return pl.pallas_call(
        functools.partial(_attn_dense_kernel, scale=128 ** -0.5, has_ctx=True),
        grid=(N_SAMPLE_SEQ, 4, nq),
        in_specs=[
            pl.BlockSpec((tq, 256), lambda b, hp, qi: (qoff + b * nq + qi, hp)),
            pl.BlockSpec((SAMPLE_LEN, 256), lambda b, hp, qi: (soff + b, hp)),
            pl.BlockSpec((SAMPLE_LEN, 128), lambda b, hp, qi: (soff + b, hp)),
            pl.BlockSpec((PAST, 256), lambda b, hp, qi: (b, hp)),
            pl.BlockSpec((PAST, 128), lambda b, hp, qi: (b, hp)),
        ],
        out_specs=pl.BlockSpec((tq, 128), lambda b, hp, qi: (b * nq + qi, hp)),
        out_shape=jax.ShapeDtypeStruct((T_S, 512), BF16),
        compiler_params=_cparams(("arbitrary", "arbitrary", "arbitrary"), 48 * 1024 * 1024),
        name="mla_attn_sample",
    )(qb, kb, vb, kc, vc)


def _group_mean_sq(x, width):
    gi = lax.broadcasted_iota(I32, (width, width), 0) // 64
    gj = lax.broadcasted_iota(I32, (width, width), 1) // 64
    bd = jnp.where(gi == gj, 1.0, 0.0).astype(BF16)
    hi, lo = _split_bf16(x * x)
    return (_dot(hi, bd) + _dot(lo, bd)) * (1.0 / 64.0)


def _dup64(x, lane):
    r = pltpu.roll(x, 64, 1)
    return jnp.concatenate([jnp.where(lane < 64, x, r), jnp.where(lane >= 64, x, r)], axis=-1)


def _odd_prep_kernel(q_ref, kv_ref, gq_ref, gk_ref, c_ref, sa_ref, sb_ref, q_out, kd_out, vd_out, k_out, v_out):
    ct, sat, sbt = c_ref[...], sa_ref[...], sb_ref[...]
    qd = q_ref[...]
    qn = qd * lax.rsqrt(_group_mean_sq(qd, 512) + EPS)
    for p in range(4):
        sl = slice(p * 128, (p + 1) * 128)
        q_out[:, sl] = _rope(qn[:, sl] * gq_ref[...], ct, sat, sbt).astype(q_out.dtype)
    kd = kv_ref[:, 0:128]
    vd = kv_ref[:, 128:256]
    kn = kd * lax.rsqrt(_group_mean_sq(kd, 128) + EPS) * gk_ref[...]
    k_out[...] = kn
    v_out[...] = vd
    lane = lax.broadcasted_iota(I32, kd.shape, 1)
    kd_out[...] = _dup64(_rope(kn, ct, sat, sbt), lane).astype(kd_out.dtype)
    vd_out[...] = _dup64(vd, lane).astype(vd_out.dtype)


def odd_prep_call(proj, gq2, gk2, tables):
    row = lambda i: (i, 0)
    const = lambda i: (0, 0)
    return pl.pallas_call(
        _odd_prep_kernel,
        grid=(NBLK,),
        in_specs=[
            pl.BlockSpec((TM, 512), lambda i: (i, 1)),
            pl.BlockSpec((TM, 256), lambda i: (i, 4)),
            pl.BlockSpec((1, 128), const),
            pl.BlockSpec((1, 128), const),
            pl.BlockSpec((TM, 128), row),
            pl.BlockSpec((TM, 128), row),
            pl.BlockSpec((TM, 128), row),
        ],
        out_specs=[
            pl.BlockSpec((TM, 512), row),
            pl.BlockSpec((TM, 256), row),
            pl.BlockSpec((TM, 256), row),
            pl.BlockSpec((TM, 128), row),
            pl.BlockSpec((TM, 128), row),
        ],
        out_shape=[
            jax.ShapeDtypeStruct((T, 512), BF16),
            jax.ShapeDtypeStruct((T, 256), BF16),
            jax.ShapeDtypeStruct((T, 256), BF16),
            jax.ShapeDtypeStruct((T, 128), F32),
            jax.ShapeDtypeStruct((T, 128), F32),
        ],
        compiler_params=_cparams(("arbitrary",)),
        name="odd_prep",
    )(proj, proj, gq2, gk2, *tables)


def _sink_heads(q_ref, sink_ref, o_ref, score_fn, value_fn, scale):
    lane = lax.broadcasted_iota(I32, (q_ref.shape[0], 128), 1)
    for p in range(4):
        qp = q_ref[:, p * 128:(p + 1) * 128]
        halves = []
        for par in range(2):
            h = 2 * p + par
            kv = h // 4
            qm = jnp.where((lane >= 64) == (par == 1), qp, jnp.zeros_like(qp))
            ss = score_fn(qm, kv)
            sk = sink_ref[h]
            m = jnp.maximum(functools.reduce(jnp.maximum, [jnp.max(s, axis=-1, keepdims=True) for s in ss]), sk)
            ps = [jnp.exp(s - m) for s in ss]
            l = functools.reduce(jnp.add, [jnp.sum(pp, axis=-1, keepdims=True) for pp in ps]) + jnp.exp(sk - m)
            halves.append(value_fn(ps, kv) / l)
        o_ref[:, p * 128:(p + 1) * 128] = jnp.where(lane < 64, halves[0], halves[1]).astype(o_ref.dtype)


def _win_prompt_kernel(sink_ref, q_ref, k_ref, v_ref, o_ref, *, scale):
    def score_fn(qm, kv):
        return [_dot_nt(qm, k_ref[:, kv * 128:(kv + 1) * 128]) * scale]

    def value_fn(ps, kv):
        return _dot(ps[0].astype(BF16), v_ref[:, kv * 128:(kv + 1) * 128])

    _sink_heads(q_ref, sink_ref, o_ref, score_fn, value_fn, scale)


def win_prompt_call(sink, q, kd, vd):
    blk = lambda b: (b, 0)
    return pl.pallas_call(
        functools.partial(_win_prompt_kernel, scale=64 ** -0.5),
        grid=(N_PROMPT_SEQ,),
        in_specs=[
            pl.BlockSpec(memory_space=pltpu.SMEM),
            pl.BlockSpec((PROMPT_LEN, 512), blk),
            pl.BlockSpec((PROMPT_LEN, 256), blk),
            pl.BlockSpec((PROMPT_LEN, 256), blk),
        ],
        out_specs=pl.BlockSpec((PROMPT_LEN, 512), blk),
        out_shape=jax.ShapeDtypeStruct((T_P, 512), BF16),
        compiler_params=_cparams(("arbitrary",)),
        name="win_prompt",
    )(sink, q, kd, vd)


WIN = 128


def _win_sample_kernel(sink_ref, q_ref, k0, k1, k2, v0, v1, v2, kc_ref, vc_ref, o_ref, *, scale):
    qi = pl.program_id(1)
    start = qi * WIN
    r = lax.broadcasted_iota(I32, (WIN, 3 * WIN), 0)
    c = lax.broadcasted_iota(I32, (WIN, 3 * WIN), 1)
    kpos = start - WIN + c
    valid = (c - r >= 0) & (c - r <= 2 * WIN) & (kpos >= 0) & (kpos < SAMPLE_LEN)

    def score_fn(qm, kv):
        sl = slice(kv * 128, (kv + 1) * 128)
        kloc = jnp.concatenate([k0[:, sl], k1[:, sl], k2[:, sl]], axis=0)
        s_loc = jnp.where(valid, _dot_nt(qm, kloc) * scale, NEG_INF)
        return [s_loc, _dot_nt(qm, kc_ref[:, sl]) * scale]

    def value_fn(ps, kv):
        sl = slice(kv * 128, (kv + 1) * 128)
        vloc = jnp.concatenate([v0[:, sl], v1[:, sl], v2[:, sl]], axis=0)
        return _dot(ps[0].astype(BF16), vloc) + _dot(ps[1].astype(BF16), vc_ref[:, sl])

    _sink_heads(q_ref, sink_ref, o_ref, score_fn, value_fn, scale)


def win_sample_call(sink, q, kpad, vpad, kc, vc):
    nq = SAMPLE_LEN // WIN
    qoff = T_P // WIN
    per = nq + 2
    loc = lambda d: (lambda b, qi: (b * per + qi + d, 0))
    return pl.pallas_call(
        functools.partial(_win_sample_kernel, scale=64 ** -0.5),
        grid=(N_SAMPLE_SEQ, nq),
        in_specs=[
            pl.BlockSpec(memory_space=pltpu.SMEM),
            pl.BlockSpec((WIN, 512), lambda b, qi: (qoff + b * nq + qi, 0)),
            pl.BlockSpec((WIN, 256), loc(0)),
            pl.BlockSpec((WIN, 256), loc(1)),
            pl.BlockSpec((WIN, 256), loc(2)),
            pl.BlockSpec((WIN, 256), loc(0)),
            pl.BlockSpec((WIN, 256), loc(1)),
            pl.BlockSpec((WIN, 256), loc(2)),
            pl.BlockSpec((PAST, 256), lambda b, qi: (b, 0)),
            pl.BlockSpec((PAST, 256), lambda b, qi: (b, 0)),
        ],
        out_specs=pl.BlockSpec((WIN, 512), lambda b, qi: (b * nq + qi, 0)),
        out_shape=jax.ShapeDtypeStruct((T_S, 512), BF16),
        compiler_params=_cparams(("arbitrary", "arbitrary")),
        name="win_sample",
    )(sink, q, kpad, kpad, kpad, vpad, vpad, vpad, kc, vc)


def _pool_kernel(xp_ref, xc_ref, xn_ref, w_ref, sc_ref, o_ref):
    i = pl.program_id(0)
    jj = (i - NBLK_P) % SBLK
    first = jnp.logical_or(i < NBLK_P, jj == 0)
    last = jnp.logical_or(i < NBLK_P, jj == SBLK - 1)
    ext = TM + 2 * POOL_HALO
    r = lax.broadcasted_iota(I32, (TM, ext), 0)
    c = lax.broadcasted_iota(I32, (TM, ext), 1) - POOL_HALO
    ok = jnp.logical_and(jnp.logical_or(c >= 0, jnp.logical_not(first)), jnp.logical_or(c < TM, jnp.logical_not(last)))
    rr = lax.broadcasted_iota(I32, (TM, 1), 0)
    parts = []
    for g, w in enumerate(POOL_WINDOWS):
        sl = slice(g * 128, (g + 1) * 128)
        x = xc_ref[:, sl]
        xe = jnp.concatenate([xp_ref[:, sl], x, xn_ref[:, sl]], axis=0)
        lo = r - w // 2
        band = (c >= lo) & (c < lo + w) & ok
        a = jnp.where(band, 1.0, 0.0).astype(BF16)
        lo1 = rr - w // 2
        lo_c = jnp.where(first, jnp.maximum(lo1, 0), lo1)
        hi_c = jnp.where(last, jnp.minimum(lo1 + w, TM), lo1 + w)
        cnt = (hi_c - lo_c).astype(F32)
        hi_x, lo_x = _split_bf16(xe)
        pooled = (_dot(a, hi_x) + _dot(a, lo_x)) / cnt
        y = _dot((pooled - x).astype(BF16), w_ref[g])
        parts.append(y)
    o_ref[...] = (jnp.concatenate(parts, axis=-1) * sc_ref[...]).astype(o_ref.dtype)


def pool_call(proj, w_pool, scale):
    hb = TM // POOL_HALO
    nh = T // POOL_HALO
    return pl.pallas_call(
        _pool_kernel,
        grid=(NBLK,),
        in_specs=[
            pl.BlockSpec((POOL_HALO, 512), lambda i: (jnp.maximum(i * hb - 1, 0), 0)),
            pl.BlockSpec((TM, 512), lambda i: (i, 0)),
            pl.BlockSpec((POOL_HALO, 512), lambda i: (jnp.minimum((i + 1) * hb, nh - 1), 0)),
            pl.BlockSpec((4, 128, 128), lambda i: (0, 0, 0)),
            pl.BlockSpec((1, 512), lambda i: (0, 0)),
        ],
        out_specs=pl.BlockSpec((TM, 512), lambda i: (i, 0)),
        out_shape=jax.ShapeDtypeStruct((T, 512), BF16),
        compiler_params=_cparams(("arbitrary",)),
        name="pool",
    )(proj, proj, proj, w_pool, scale)


TOPK = 16
TK_TM = 128


def _top16(s, key, aux=None):
    cols = s.shape[1]
    r16 = lax.broadcasted_iota(I32, (TOPK, cols), 0)
    vals = jnp.zeros((TOPK, cols), F32)
    outs = jnp.zeros((TOPK, cols), F32)
    for r in range(TOPK):
        m = jnp.max(s, axis=0, keepdims=True)
        p = jnp.min(jnp.where(s == m, key, np.float32(1e9)), axis=0, keepdims=True)
        hit = key == p
        o = p if aux is None else jnp.max(jnp.where(hit, aux, -1.0), axis=0, keepdims=True)
        vals = jnp.where(r16 == r, m, vals)
        outs = jnp.where(r16 == r, o, outs)
        s = jnp.where(hit, -jnp.inf, s)
    return vals, outs


def _pair_candidates(v1, i1, v2, i2):
    cols = v1.shape[1]
    n8 = lax.broadcasted_iota(I32, (8, cols), 0)
    n16 = lax.broadcasted_iota(I32, (16, cols), 0)
    f8, f16 = n8.astype(F32), n16.astype(F32)
    sc, pos, ex = [], [], []

    def add(score, position, expert, mask):
        sc.append(score if mask is None else jnp.where(mask, score, -jnp.inf))
        pos.append(position)
        ex.append(expert)

    add(v1[0:1] + v2, f16, i1[0:1] * 128.0 + i2, None)
    add(v1[1:2] + v2[0:8], 16.0 + f8, i1[1:2] * 128.0 + i2[0:8], None)
    add(v1[2:3] + v2[0:8], 32.0 + f8, i1[2:3] * 128.0 + i2[0:8], n8 < 5)
    add(v1[3:4] + v2[0:8], 48.0 + f8, i1[3:4] * 128.0 + i2[0:8], n8 < 4)
    add(v1 + v2[0:1], f16 * 16.0, i1 * 128.0 + i2[0:1], n16 >= 4)
    add(v1[0:8] + v2[1:2], f8 * 16.0 + 1.0, i1[0:8] * 128.0 + i2[1:2], n8 >= 4)
    add(v1[0:8] + v2[2:3], f8 * 16.0 + 2.0, i1[0:8] * 128.0 + i2[2:3], n8 == 4)
    return jnp.concatenate(sc, axis=0), jnp.concatenate(pos, axis=0), jnp.concatenate(ex, axis=0)


def _peer_topk_kernel(q_ref, sk_ref, idx_ref, gate_ref):
    rowf = lax.broadcasted_iota(I32, (128, q_ref.shape[0]), 0).astype(F32)
    idx_rows, gate_rows = [], []
    for h in range(8):
        s1 = _dot_nt(sk_ref[2 * h], q_ref[:, (2 * h) * 128:(2 * h + 1) * 128])
        s2 = _dot_nt(sk_ref[2 * h + 1], q_ref[:, (2 * h + 1) * 128:(2 * h + 2) * 128])
        v1, i1 = _top16(s1, rowf)
        v2, i2 = _top16(s2, rowf)
        cand, pos, expert = _pair_candidates(v1, i1, v2, i2)
        tv, te = _top16(cand, pos, expert)
        e = jnp.exp(tv - tv[0:1])
        gate_rows.append(e / jnp.sum(e, axis=0, keepdims=True))
        idx_rows.append(te)
    idx_ref[...] = (jnp.concatenate(idx_rows, axis=0).T * 4.0).astype(I32)
    gate_ref[...] = jnp.concatenate(gate_rows, axis=0).T


def peer_topk_call(q, sk):
    row = lambda i: (i, 0)
    return pl.pallas_call(
        _peer_topk_kernel,
        grid=(T // TK_TM,),
        in_specs=[pl.BlockSpec((TK_TM, 2048), row), pl.BlockSpec((16, 128, 128), lambda i: (0, 0, 0))],
        out_specs=[pl.BlockSpec((TK_TM, NPICK), row), pl.BlockSpec((TK_TM, NPICK), row)],
        out_shape=[jax.ShapeDtypeStruct((T, NPICK), I32), jax.ShapeDtypeStruct((T, NPICK), F32)],
        compiler_params=_cparams(("arbitrary",)),
        name="peer_topk",
    )(q, sk)


def _gather_token(idx_smem, tab_ref, stage_ref, g, j):
    for k in range(NPICK):
        e4 = pl.multiple_of(idx_smem[g * PEER_TG + j, k], 4)
        stage_ref[pl.ds(j * 512 + 4 * k, 4), :] = tab_ref[pl.ds(e4, 4), :]


def _idx_copy(idx_hbm, idx_smem, sems, block, g):
    row0 = block * PEER_TB + g * PEER_TG
    return pltpu.make_async_copy(
        idx_hbm.at[pl.ds(row0, PEER_TG)], idx_smem.at[pl.ds(g * PEER_TG, PEER_TG)], sems.at[g])


def _gather_schedule(idx_hbm, idx_smem, sems, tab_ref, stage, begin, token, end):
    ngroup = PEER_TB // PEER_TG
    i = pl.program_id(0)
    last = pl.num_programs(0) - 1
    nxt = jnp.minimum(i + 1, last)
    nxt2 = jnp.minimum(i + 2, last)

    @pl.when(i == 0)
    def _():
        for g in range(ngroup):
            _idx_copy(idx_hbm, idx_smem, sems, 0, g).start()
        _idx_copy(idx_hbm, idx_smem, sems, 0, 0).wait()
        for j in range(PEER_TG):
            _gather_token(idx_smem, tab_ref, stage, 0, j)
        _idx_copy(idx_hbm, idx_smem, sems, nxt, 0).start()

    pending = None
    for g in range(ngroup):
        base = g * PEER_TG
        gn = (g + 1) % ngroup
        state = begin(base)
        _idx_copy(idx_hbm, idx_smem, sems, 0, gn).wait()
        for j in range(PEER_TG):
            state = token(state, stage, base, j)
            _gather_token(idx_smem, tab_ref, stage, gn, j)
            if j == 1 and pending is not None:
                end(*pending)
        _idx_copy(idx_hbm, idx_smem, sems, nxt if gn else nxt2, gn).start()
        pending = (state, base)
    end(*pending)

    @pl.when(i == last)
    def _():
        for g in range(ngroup):
            _idx_copy(idx_hbm, idx_smem, sems, 0, g).wait()


def _staged_tiles(stage_ref, j):
    return pltpu.bitcast(stage_ref[pl.ds(j * 512, 512), :], BF16)


def _peer_u_kernel(idx_hbm, x_ref, gate_ref, tab_ref, act_ref, idx_smem, stage, sems):
    col = lax.broadcasted_iota(I32, (16, 1024), 1)
    row = lax.broadcasted_iota(I32, (16, 1024), 0)
    diag = (col % 8) == (row % 8)
    gi = lax.broadcasted_iota(I32, (1024, 128), 0)
    gj = lax.broadcasted_iota(I32, (1024, 128), 1)
    gsum = jnp.where(gi // 8 == gj, 1.0, 0.0).astype(BF16)
    r8 = lax.broadcasted_iota(I32, (8, 1024), 0)

    def begin(base):
        return jnp.zeros((8, 1024), F32)

    def token(zs, stage_ref, base, j):
        xh, xl = _split_bf16(x_ref[base + j])
        lhs = jnp.concatenate([xh, xl], axis=0)
        z = _dot_nt(lhs, _staged_tiles(stage_ref, j))
        zr = jnp.sum(jnp.where(diag, z, 0.0), axis=0, keepdims=True)
        return jnp.where(r8 == j, zr, zs)

    def end(zs, base):
        zh, zl = _split_bf16(zs)
        sc = _dot(zh, gsum) + _dot(zl, gsum)
        gate = gate_ref[pl.ds(base, PEER_TG), :]
        act = 0.5 * sc * (1.0 + lax.erf(sc * np.float32(1.0 / np.sqrt(2.0)))) * gate
        act_ref[pl.ds(base, PEER_TG), :] = act

    _gather_schedule(idx_hbm, idx_smem, sems, tab_ref, stage, begin, token, end)


def _peer_v_kernel(idx_hbm, act_ref, tab_ref, out_ref, idx_smem, stage, sems):
    ei = lax.broadcasted_iota(I32, (128, 1024), 0)
    ej = lax.broadcasted_iota(I32, (128, 1024), 1)
    expand = jnp.where(ej // 8 == ei, 1.0, 0.0).astype(BF16)
    col = lax.broadcasted_iota(I32, (8, 1024), 1)
    row = lax.broadcasted_iota(I32, (8, 1024), 0)
    diag = (col % 8) == row

    def begin(base):
        ah, al = _split_bf16(act_ref[pl.ds(base, PEER_TG), :])
        return _dot(ah, expand), _dot(al, expand)

    def token(rep, stage_ref, base, j):
        lh = jnp.where(diag, rep[0][j:j + 1, :], 0.0).astype(BF16)
        ll = jnp.where(diag, rep[1][j:j + 1, :], 0.0).astype(BF16)
        o = _dot(jnp.concatenate([lh, ll], axis=0), _staged_tiles(stage_ref, j))
        out_ref[base + j] = o[0:8] + o[8:16]
        return rep

    def end(rep, base):
        pass

    _gather_schedule(idx_hbm, idx_smem, sems, tab_ref, stage, begin, token, end)


def _gather_scratch():
    return [
        pltpu.SMEM((PEER_TB, NPICK), I32),
        pltpu.VMEM((PEER_TG * 512, 128), jnp.uint32),
        pltpu.SemaphoreType.DMA((PEER_TB // PEER_TG,)),
    ]


def _pack_table(tab):
    t = lax.bitcast_convert_type(tab.astype(BF16), jnp.uint16).reshape(P_EXPERTS, 4, 2, 128).astype(jnp.uint32)
    return (t[:, :, 0, :] | (t[:, :, 1, :] << 16)).reshape(P_EXPERTS * 4, 128)


def peer_u_call(idx, x3, gate, tab):
    tb = PEER_TB
    return pl.pallas_call(
        _peer_u_kernel,
        grid=(T // tb,),
        in_specs=[
            pl.BlockSpec(memory_space=pl.ANY),
            pl.BlockSpec((tb, 8, 128), lambda i: (i, 0, 0)),
            pl.BlockSpec((tb, NPICK), lambda i: (i, 0)),
            pl.BlockSpec((P_EXPERTS * 4, 128), lambda i: (0, 0), pipeline_mode=pl.Buffered(1)),
        ],
        out_specs=pl.BlockSpec((tb, NPICK), lambda i: (i, 0)),
        out_shape=jax.ShapeDtypeStruct((T, NPICK), F32),
        scratch_shapes=_gather_scratch(),
        compiler_params=_cparams(("arbitrary",), VMEM_LIMIT),
        name="peer_u",
    )(idx, x3, gate, tab)


def peer_v_call(idx, act, tab):
    tb = PEER_TB
    return pl.pallas_call(
        _peer_v_kernel,
        grid=(T // tb,),
        in_specs=[
            pl.BlockSpec(memory_space=pl.ANY),
            pl.BlockSpec((tb, NPICK), lambda i: (i, 0)),
            pl.BlockSpec((P_EXPERTS * 4, 128), lambda i: (0, 0), pipeline_mode=pl.Buffered(1)),
        ],
        out_specs=pl.BlockSpec((tb, 8, 128), lambda i: (i, 0, 0)),
        out_shape=jax.ShapeDtypeStruct((T, 8, 128), F32),
        scratch_shapes=_gather_scratch(),
        compiler_params=_cparams(("arbitrary",), VMEM_LIMIT),
        name="peer_v",
    )(idx, act, tab)


def _even_in_weight(w):
    qa, ka, va, ra, glf, glb, cq, ckv, kr = jnp.split(w, np.cumsum([256, 256, 512, 512, 16, 16, 256, 256, 64])[:-1].tolist(), axis=-1)
    pad = jnp.zeros((D, 32), w.dtype)
    return jnp.concatenate([qa, ka, va, ra, cq, ckv, glf, glb, pad, kr], axis=-1).astype(BF16)


def _gate_up_pad(w_gu):
    z = jnp.zeros((2, 128, 256), F32)
    z = z.at[0, 0:16].set(w_gu[0])
    z = z.at[1, 16:32].set(w_gu[1])
    return z.astype(BF16)


def _uk_pad(w_uk):
    w = w_uk.reshape(256, 8, 64)
    return jnp.concatenate([w, jnp.zeros_like(w)], axis=-1).reshape(256, 1024).astype(BF16)


def _dup_heads(x):
    a, b = x[:, :64], x[:, 64:]
    return jnp.concatenate([a, a, b, b], axis=-1).astype(BF16)


def _pad_latent(x):
    x = x.reshape(N_SAMPLE_SEQ, SAMPLE_LEN, 256)
    x = jnp.pad(x, ((0, 0), (WIN, WIN), (0, 0)))
    return x.reshape(N_SAMPLE_SEQ * (SAMPLE_LEN + 2 * WIN), 256)


def kernel(x_prompt, x_sample, state_gla, cache_mla_ckv, cache_mla_krope, cache_win_kv, c, c_ctx, g_norm, w_ada, b_ada, w_in_even, w_gate_up, b_gate_up, g_gla_out, g_mla_cq, g_mla_ckv, w_mla_uq, w_mla_uk, w_mla_uv, g_mla_qn, g_mla_kn, w_out_even, w_in_odd, w_pool, pool_scale, g_win_qn, g_win_kn, win_sink, w_out_odd, peer_wq, peer_subkeys, peer_u, peer_v):
    depth = w_ada.shape[0]
    x = jnp.concatenate([x_prompt.reshape(T_P, D), x_sample.reshape(T_S, D)], axis=0)
    cond8 = jnp.concatenate([c_ctx[None], c, jnp.zeros((3, D), F32)], axis=0)
    mods_all = ada_call(cond8, w_ada, b_ada).reshape(depth, 8, 6, D)
    tables_b = _rope_tables("upper")
    tables_d = _rope_tables("all")

    gla_states, mla_ckv, mla_kr, win_kv = [], [], [], []
    for l in range(depth):
        mods = mods_all[l]
        if l % 2 == 0:
            e = l // 2
            (proj,) = modproj_call(x, g_norm[l, 0], mods, _even_in_weight(w_in_even[e]), 0, F32, False)
            s0 = jnp.concatenate([jnp.zeros((N_PROMPT_SEQ, 2, 4, 64, 128), F32), state_gla[:, e]], axis=0)
            s0t = jnp.swapaxes(s0.reshape(-1, 2, 256, 128), -1, -2)
            o_f, o_b, sfin = gla_call(proj, _gate_up_pad(w_gate_up[e]), b_gate_up[e].reshape(2, 1, 256), s0t)
            sfin = jnp.swapaxes(sfin[:N_PROMPT_SEQ], -1, -2).reshape(N_PROMPT_SEQ, 2, 4, 64, 128)
            gla_states.append(sfin)
            wuk_pad = _uk_pad(w_mla_uk[e])
            wuv = w_mla_uv[e].astype(BF16)
            gkn = g_mla_kn[e].reshape(1, 128)
            qb, kb, vb, ckv_n, krf = mla_prep_call(
                proj, g_mla_cq[e].reshape(1, 256), g_mla_ckv[e].reshape(1, 256), w_mla_uq[e].astype(BF16),
                wuk_pad, wuv, g_mla_qn[e].reshape(1, 128), gkn, tables_b)
            mla_ckv.append(ckv_n[:T_P].reshape(N_PROMPT_SEQ, PROMPT_LEN, 256))
            mla_kr.append(krf[:T_P, 64:].reshape(N_PROMPT_SEQ, PROMPT_LEN, 64))
            kr_ctx = cache_mla_krope[:, e].reshape(N_SAMPLE_SEQ * PAST, 64)
            kr_pad = jnp.concatenate([jnp.zeros_like(kr_ctx), kr_ctx], axis=-1)
            kc, vc = mla_ctx_call(cache_mla_ckv[:, e].reshape(N_SAMPLE_SEQ * PAST, 256), kr_pad, wuk_pad, wuv, gkn)
            ob = jnp.concatenate([mla_attn_prompt_call(qb, kb, vb), mla_attn_sample_call(qb, kb, vb, kc, vc)], axis=0)
            x = outproj_call(x, o_f, o_b, proj, 2, g_gla_out[e].reshape(1, 128), ob, w_out_even[e].astype(BF16), mods, True)
        else:
            o = l // 2
            (proj,) = modproj_call(x, g_norm[l, 0], mods, w_in_odd[o].astype(BF16), 0, F32, False)
            oc = pool_call(proj, w_pool[o].astype(BF16), pool_scale[o].reshape(1, 512))
            gq2 = jnp.tile(g_win_qn[o], 2).reshape(1, 128)
            gk2 = jnp.tile(g_win_kn[o], 2).reshape(1, 128)
            qw, kd, vd, kn, vn = odd_prep_call(proj, gq2, gk2, tables_d)
            k_p = kn[:T_P].reshape(N_PROMPT_SEQ, PROMPT_LEN, 2, 64)
            v_p = vn[:T_P].reshape(N_PROMPT_SEQ, PROMPT_LEN, 2, 64)
            win_kv.append(jnp.stack([k_p, v_p], axis=1))
            kc = _dup_heads(cache_win_kv[:, o, 0].reshape(N_SAMPLE_SEQ * PAST, 128))
            vc = _dup_heads(cache_win_kv[:, o, 1].reshape(N_SAMPLE_SEQ * PAST, 128))
            od = jnp.concatenate([
                win_prompt_call(win_sink[o], qw, kd, vd),
                win_sample_call(win_sink[o], qw, _pad_latent(kd[T_P:]), _pad_latent(vd[T_P:]), kc, vc)], axis=0)
            x = outproj_call(x, oc, oc, proj, 0, jnp.ones((1, 128), F32), od, w_out_odd[o].astype(BF16), mods, False)
        q, h = modproj_call(x, g_norm[l, 1], mods, peer_wq[l].astype(BF16), 3, BF16, True)
        idx, gate = peer_topk_call(q, peer_subkeys[l].reshape(16, 128, 128).astype(BF16))
        tab_u = _pack_table(peer_u[l])
        tab_v = _pack_table(peer_v[l])
        act = peer_u_call(idx, h.reshape(T, 8, 128), gate, tab_u)
        y = peer_v_call(idx, act, tab_v).reshape(T, D)
        x = resid_call(x, y, mods, 5)

    return (
        x[:T_P].reshape(N_PROMPT_SEQ, PROMPT_LEN, D),
        x[T_P:].reshape(N_SAMPLE_SEQ, SAMPLE_LEN, D),
        jnp.stack(gla_states, axis=1),
        jnp.stack(mla_ckv, axis=1),
        jnp.stack(mla_kr, axis=1),
        jnp.stack(win_kv, axis=1),
    )
```

```python
import functools

import jax
import jax.numpy as jnp
import numpy as np
from jax import lax
from jax.experimental import pallas as pl
from jax.experimental.pallas import tpu as pltpu

F32 = jnp.float32
BF16 = jnp.bfloat16
I32 = jnp.int32

D = 1024
N_PROMPT_SEQ = 32
PROMPT_LEN = 256
N_SAMPLE_SEQ = 4
SAMPLE_LEN = 4096
PAST = 512
T_P = N_PROMPT_SEQ * PROMPT_LEN
T_S = N_SAMPLE_SEQ * SAMPLE_LEN
T = T_P + T_S
GRID_W = 64
EPS = 1e-6
NEG_INF = -1e30
ROPE_BASE = 10000.0

TM = 256
NBLK = T // TM
NBLK_P = T_P // TM
SBLK = SAMPLE_LEN // TM

EVEN_W = 2176
ODD_W = 1280
POOL_WINDOWS = (2, 4, 8, 16)
POOL_HALO = 8

P_EXPERTS = 16384
NPICK = 128
PEER_TB = 64
PEER_TG = 8
VMEM_LIMIT = 56 * 1024 * 1024


def _cparams(sem, vmem=None):
    return pltpu.CompilerParams(dimension_semantics=sem, vmem_limit_bytes=vmem)


def _mod_group(i):
    return jnp.where(i < NBLK_P, 0, 1 + (i - NBLK_P) // SBLK)


def _split_bf16(x):
    hi = x.astype(BF16)
    lo = (x - hi.astype(F32)).astype(BF16)
    return hi, lo


def _dot(a, b):
    return jnp.dot(a, b, preferred_element_type=F32)


def _dot_nt(a, b):
    return lax.dot_general(a, b, (((1,), (1,)), ((), ())), preferred_element_type=F32)


def _rms(x, g):
    ms = jnp.mean(x * x, axis=-1, keepdims=True)
    return x * lax.rsqrt(ms + EPS) * g


def _ada_kernel(c_ref, w_ref, b_ref, o_ref):
    c = c_ref[...]
    s = c / (1.0 + jnp.exp(-c))
    o_ref[0] = _dot(s.astype(BF16), w_ref[0].astype(BF16)) + b_ref[0]


def ada_call(cond8, w_ada, b_ada):
    depth, _, n6 = w_ada.shape
    tn = 1536
    return pl.pallas_call(
        _ada_kernel,
        grid=(depth, n6 // tn),
        in_specs=[
            pl.BlockSpec((8, D), lambda l, j: (0, 0)),
            pl.BlockSpec((1, D, tn), lambda l, j: (l, 0, j)),
            pl.BlockSpec((1, 1, tn), lambda l, j: (l, 0, j)),
        ],
        out_specs=pl.BlockSpec((1, 8, tn), lambda l, j: (l, 0, j)),
        out_shape=jax.ShapeDtypeStruct((depth, 8, n6), F32),
        compiler_params=_cparams(("arbitrary", "arbitrary"), 40 * 1024 * 1024),
        name="ada",
    )(cond8, w_ada, b_ada.reshape(depth, 1, n6))


def _modproj_kernel(x_ref, g_ref, mod_ref, w_ref, o_ref, *h_ref, a):
    x = x_ref[...]
    m = mod_ref[0]
    h = _rms(x, g_ref[...]) * (1.0 + m[a + 1:a + 2]) + m[a:a + 1]
    o_ref[...] = _dot(h.astype(BF16), w_ref[...]).astype(o_ref.dtype)
    if h_ref:
        h_ref[0][...] = h


def modproj_call(x, g, mods, w, a, out_dtype, with_h):
    n = w.shape[1]
    out_shape = [jax.ShapeDtypeStruct((T, n), out_dtype)]
    out_specs = [pl.BlockSpec((TM, n), lambda i: (i, 0))]
    if with_h:
        out_shape.append(jax.ShapeDtypeStruct((T, D), F32))
        out_specs.append(pl.BlockSpec((TM, D), lambda i: (i, 0)))
    return pl.pallas_call(
        functools.partial(_modproj_kernel, a=a),
        grid=(NBLK,),
        in_specs=[
            pl.BlockSpec((TM, D), lambda i: (i, 0)),
            pl.BlockSpec((1, D), lambda i: (0, 0)),
            pl.BlockSpec((1, 6, D), lambda i: (_mod_group(i), 0, 0)),
            pl.BlockSpec((D, n), lambda i: (0, 0)),
        ],
        out_specs=out_specs,
        out_shape=out_shape,
        compiler_params=_cparams(("arbitrary",), 40 * 1024 * 1024),
        name="modproj",
    )(x, g.reshape(1, D), mods, w)


def _resid_kernel(x_ref, y_ref, mod_ref, o_ref, *, a):
    o_ref[...] = x_ref[...] + mod_ref[0][a:a + 1] * y_ref[...]


def resid_call(x, y, mods, a):
    return pl.pallas_call(
        functools.partial(_resid_kernel, a=a),
        grid=(NBLK,),
        in_specs=[
            pl.BlockSpec((TM, D), lambda i: (i, 0)),
            pl.BlockSpec((TM, D), lambda i: (i, 0)),
            pl.BlockSpec((1, 6, D), lambda i: (_mod_group(i), 0, 0)),
        ],
        out_specs=pl.BlockSpec((TM, D), lambda i: (i, 0)),
        out_shape=jax.ShapeDtypeStruct((T, D), F32),
        compiler_params=_cparams(("arbitrary",)),
        name="resid",
    )(x, y, mods)


def _outproj_kernel(x_ref, a_ref, a2_ref, r_ref, ggo_ref, b_ref, w_ref, mod_ref, o_ref, *, gla_post):
    if gla_post:
        o = a_ref[...] + a2_ref[...]
        r = r_ref[...]
        parts = []
        for h in range(4):
            seg = _rms(o[:, h * 128:(h + 1) * 128], ggo_ref[...])
            rr = r[:, h * 128:(h + 1) * 128]
            parts.append(seg * (rr / (1.0 + jnp.exp(-rr))))
        mix_a = jnp.concatenate(parts, axis=-1)
    else:
        mix_a = a_ref[...]
    w = w_ref[...]
    y = _dot(mix_a.astype(BF16), w[0:512]) + _dot(b_ref[...].astype(BF16), w[512:1024])
    o_ref[...] = x_ref[...] + mod_ref[0][2:3] * y


def outproj_call(x, mix_a, mix_a2, proj, ra_col, g_go, mix_b, w, mods, gla_post):
    row = lambda i: (i, 0)
    return pl.pallas_call(
        functools.partial(_outproj_kernel, gla_post=gla_post),
        grid=(NBLK,),
        in_specs=[
            pl.BlockSpec((TM, D), row),
            pl.BlockSpec((TM, 512), row),
            pl.BlockSpec((TM, 512), row),
            pl.BlockSpec((TM, 512), lambda i: (i, ra_col)),
            pl.BlockSpec((1, 128), lambda i: (0, 0)),
            pl.BlockSpec((TM, 512), row),
            pl.BlockSpec((D, D), lambda i: (0, 0)),
            pl.BlockSpec((1, 6, D), lambda i: (_mod_group(i), 0, 0)),
        ],
        out_specs=pl.BlockSpec((TM, D), row),
        out_shape=jax.ShapeDtypeStruct((T, D), F32),
        compiler_params=_cparams(("arbitrary",), 40 * 1024 * 1024),
        name="outproj",
    )(x, mix_a, mix_a2, proj, g_go, mix_b, w, mods)


GLA_CHUNK = 64


def _log_sigmoid(z):
    return jnp.minimum(z, 0.0) - jnp.log1p(jnp.exp(-jnp.abs(z)))


def _gla_direction(qk_ref, v_ref, gl_ref, wgu, bgu, st_ref, o_ref, reverse):
    ri = lax.broadcasted_iota(I32, (GLA_CHUNK, GLA_CHUNK), 0)
    ci = lax.broadcasted_iota(I32, (GLA_CHUNK, GLA_CHUNK), 1)
    keep = (ci >= ri) if reverse else (ci <= ri)
    tri = jnp.where(keep, 1.0, 0.0).astype(BF16)
    lane = lax.broadcasted_iota(I32, (GLA_CHUNK, 256), 1)
    lane_s = lax.broadcasted_iota(I32, (128, 256), 1)
    nchunk = TM // GLA_CHUNK
    order = range(nchunk - 1, -1, -1) if reverse else range(nchunk)
    for c in order:
        r0 = c * GLA_CHUNK
        q = qk_ref[r0:r0 + GLA_CHUNK, 0:256] * 0.125
        k = qk_ref[r0:r0 + GLA_CHUNK, 256:512]
        v = v_ref[r0:r0 + GLA_CHUNK, :]
        z = _dot(gl_ref[r0:r0 + GLA_CHUNK, :].astype(BF16), wgu) + bgu
        la = _log_sigmoid(z) * (1.0 / 16.0)
        la_hi, la_lo = _split_bf16(la)
        b = _dot(tri, la_hi) + _dot(tri, la_lo)
        b_tot = b[0:1, :] if reverse else b[GLA_CHUNK - 1:GLA_CHUNK, :]
        q_dec = q * jnp.exp(b)
        k_intra = (k * jnp.exp(-b)).astype(BF16)
        k_state = (k * jnp.exp(b_tot - b)).astype(BF16)
        st = st_ref[...]
        st_b = st.astype(BF16)
        v_b = v.astype(BF16)
        vt = v.T.astype(BF16)
        outs = []
        upd = jnp.zeros((128, 256), F32)
        for h in range(4):
            qm = jnp.where(lane // 64 == h, q_dec, 0.0).astype(BF16)
            att = jnp.where(keep, _dot_nt(qm, k_intra), 0.0)
            o_h = _dot(att.astype(BF16), v_b[:, h * 128:(h + 1) * 128]) + _dot_nt(qm, st_b)
            outs.append(o_h)
            u_h = _dot(vt[h * 128:(h + 1) * 128, :], k_state)
            upd = jnp.where(lane_s // 64 == h, u_h, upd)
        st_ref[...] = st * jnp.exp(b_tot) + upd
        o_ref[r0:r0 + GLA_CHUNK, :] = jnp.concatenate(outs, axis=-1)


def _gla_kernel(qkf, vf, glf, qkb, vb, glb, wgu_ref, bgu_ref, s0_ref, of_ref, ob_ref, sfin_ref, sf_scr, sb_scr):
    i = pl.program_id(0)
    jj = (i - NBLK_P) % SBLK
    first = jnp.logical_or(i < NBLK_P, jj == 0)
    last = jnp.logical_or(i < NBLK_P, jj == SBLK - 1)

    @pl.when(first)
    def _():
        sf_scr[...] = s0_ref[0, 0]
        sb_scr[...] = s0_ref[0, 1]

    _gla_direction(qkf, vf, glf, wgu_ref[0], bgu_ref[0], sf_scr, of_ref, False)
    _gla_direction(qkb, vb, glb, wgu_ref[1], bgu_ref[1], sb_scr, ob_ref, True)

    @pl.when(last)
    def _():
        sfin_ref[0, 0] = sf_scr[...]
        sfin_ref[0, 1] = sb_scr[...]


def _seq_of_block(i):
    return jnp.where(i < NBLK_P, i, NBLK_P + (i - NBLK_P) // SBLK)


def _rev_block(i):
    s = (i - NBLK_P) // SBLK
    jj = (i - NBLK_P) % SBLK
    return jnp.where(i < NBLK_P, i, NBLK_P + s * SBLK + (SBLK - 1 - jj))


def gla_call(proj, wgu_pad, bgu, s0t):
    nseq = N_PROMPT_SEQ + N_SAMPLE_SEQ
    fwd = lambda c: (lambda i: (i, c))
    bwd = lambda c: (lambda i: (_rev_block(i), c))
    return pl.pallas_call(
        _gla_kernel,
        grid=(NBLK,),
        in_specs=[
            pl.BlockSpec((TM, 512), fwd(0)),
            pl.BlockSpec((TM, 512), fwd(1)),
            pl.BlockSpec((TM, 128), fwd(16)),
            pl.BlockSpec((TM, 512), bwd(0)),
            pl.BlockSpec((TM, 512), bwd(1)),
            pl.BlockSpec((TM, 128), bwd(16)),
            pl.BlockSpec((2, 128, 256), lambda i: (0, 0, 0)),
            pl.BlockSpec((2, 1, 256), lambda i: (0, 0, 0)),
            pl.BlockSpec((1, 2, 128, 256), lambda i: (_seq_of_block(i), 0, 0, 0)),
        ],
        out_specs=[
            pl.BlockSpec((TM, 512), fwd(0)),
            pl.BlockSpec((TM, 512), bwd(0)),
            pl.BlockSpec((1, 2, 128, 256), lambda i: (_seq_of_block(i), 0, 0, 0)),
        ],
        out_shape=[
            jax.ShapeDtypeStruct((T, 512), F32),
            jax.ShapeDtypeStruct((T, 512), F32),
            jax.ShapeDtypeStruct((nseq, 2, 128, 256), F32),
        ],
        scratch_shapes=[pltpu.VMEM((128, 256), F32), pltpu.VMEM((128, 256), F32)],
        compiler_params=_cparams(("arbitrary",), 40 * 1024 * 1024),
        name="gla",
    )(proj, proj, proj, proj, proj, proj, wgu_pad, bgu, s0t)


def _rope(x, c, sa, sb):
    return x * c + pltpu.roll(x, 112, 1) * sa + pltpu.roll(x, 16, 1) * sb


def _rope_tables(rot_lanes):
    rows = SAMPLE_LEN // GRID_W
    quarter = 16
    inv = ROPE_BASE ** (-jnp.arange(quarter, dtype=F32) / quarter)
    row = jnp.repeat(jnp.arange(rows, dtype=F32), GRID_W)
    col = jnp.tile(jnp.arange(GRID_W, dtype=F32), rows)
    ar = row[:, None] * inv
    ac = col[:, None] * inv
    ang = jnp.concatenate([ar, ar, ac, ac], axis=-1)
    cos, sin = jnp.cos(ang), jnp.sin(ang)
    seg = (np.arange(64) // 16) % 2
    sa64 = jnp.where(seg == 0, -sin, 0.0)
    sb64 = jnp.where(seg == 1, sin, 0.0)
    if rot_lanes == "upper":
        c = jnp.concatenate([jnp.ones_like(cos), cos], axis=-1)
        sa = jnp.concatenate([jnp.zeros_like(sin), sa64], axis=-1)
        sb = jnp.concatenate([jnp.zeros_like(sin), sb64], axis=-1)
    else:
        c = jnp.concatenate([cos, cos], axis=-1)
        sa = jnp.concatenate([sa64, sa64], axis=-1)
        sb = jnp.concatenate([sb64, sb64], axis=-1)

    def full(t, fill):
        t = jnp.tile(t, (N_SAMPLE_SEQ, 1))
        return jnp.concatenate([jnp.full((T_P, 128), fill, F32), t], axis=0)

    return full(c, 1.0), full(sa, 0.0), full(sb, 0.0)


def _mla_prep_kernel(*refs, with_q, norm_ckv, rope):
    it = iter(refs)
    cq_ref = next(it) if with_q else None
    ckv_ref = next(it)
    kr_ref = next(it)
    gcq_ref = next(it) if with_q else None
    gckv_ref = next(it) if norm_ckv else None
    wuq_ref = next(it) if with_q else None
    wuk_ref = next(it)
    wuv_ref = next(it)
    gqn_ref = next(it) if with_q else None
    gkn_ref = next(it)
    if rope:
        c_ref, sa_ref, sb_ref = next(it), next(it), next(it)
    q_out = next(it) if with_q else None
    k_out = next(it)
    v_out = next(it)
    ckv_out = next(it) if norm_ckv else None
    kr_out = next(it) if norm_ckv else None

    if rope:
        ct, sat, sbt = c_ref[...], sa_ref[...], sb_ref[...]
    ckv = ckv_ref[...]
    if norm_ckv:
        ckv = _rms(ckv, gckv_ref[...])
        ckv_out[...] = ckv
    lane = lax.broadcasted_iota(I32, kr_ref.shape, 1)
    krf = jnp.where(lane >= 64, kr_ref[...], 0.0)
    if norm_ckv:
        kr_out[...] = krf
    ckv_b = ckv.astype(BF16)
    knope = _dot(ckv_b, wuk_ref[...])
    v_out[...] = _dot(ckv_b, wuv_ref[...]).astype(v_out.dtype)
    if with_q:
        qb = _dot(_rms(cq_ref[...], gcq_ref[...]).astype(BF16), wuq_ref[...])
    for h in range(8):
        sl = slice(h * 128, (h + 1) * 128)
        kh = _rms(knope[:, sl] + krf, gkn_ref[...])
        if rope:
            kh = _rope(kh, ct, sat, sbt)
        k_out[:, sl] = kh.astype(k_out.dtype)
        if with_q:
            qh = _rms(qb[:, sl], gqn_ref[...])
            if rope:
                qh = _rope(qh, ct, sat, sbt)
            q_out[:, sl] = qh.astype(q_out.dtype)


def mla_prep_call(proj, gcq, gckv, wuq, wuk_pad, wuv, gqn, gkn, tables):
    row = lambda i: (i, 0)
    const = lambda i: (0, 0)
    return pl.pallas_call(
        functools.partial(_mla_prep_kernel, with_q=True, norm_ckv=True, rope=True),
        grid=(NBLK,),
        in_specs=[
            pl.BlockSpec((TM, 256), lambda i: (i, 6)),
            pl.BlockSpec((TM, 256), lambda i: (i, 7)),
            pl.BlockSpec((TM, 128), lambda i: (i, 16)),
            pl.BlockSpec((1, 256), const),
            pl.BlockSpec((1, 256), const),
            pl.BlockSpec((256, 1024), const),
            pl.BlockSpec((256, 1024), const),
            pl.BlockSpec((256, 512), const),
            pl.BlockSpec((1, 128), const),
            pl.BlockSpec((1, 128), const),
            pl.BlockSpec((TM, 128), row),
            pl.BlockSpec((TM, 128), row),
            pl.BlockSpec((TM, 128), row),
        ],
        out_specs=[
            pl.BlockSpec((TM, 1024), row),
            pl.BlockSpec((TM, 1024), row),
            pl.BlockSpec((TM, 512), row),
            pl.BlockSpec((TM, 256), row),
            pl.BlockSpec((TM, 128), row),
        ],
        out_shape=[
            jax.ShapeDtypeStruct((T, 1024), BF16),
            jax.ShapeDtypeStruct((T, 1024), BF16),
            jax.ShapeDtypeStruct((T, 512), BF16),
            jax.ShapeDtypeStruct((T, 256), F32),
            jax.ShapeDtypeStruct((T, 128), F32),
        ],
        compiler_params=_cparams(("arbitrary",), 40 * 1024 * 1024),
        name="mla_prep",
    )(proj, proj, proj, gcq, gckv, wuq, wuk_pad, wuv, gqn, gkn, *tables)


def mla_ctx_call(ckv_ctx, kr_pad, wuk_pad, wuv, gkn):
    n = ckv_ctx.shape[0]
    row = lambda i: (i, 0)
    const = lambda i: (0, 0)
    return pl.pallas_call(
        functools.partial(_mla_prep_kernel, with_q=False, norm_ckv=False, rope=False),
        grid=(n // TM,),
        in_specs=[
            pl.BlockSpec((TM, 256), row),
            pl.BlockSpec((TM, 128), row),
            pl.BlockSpec((256, 1024), const),
            pl.BlockSpec((256, 512), const),
            pl.BlockSpec((1, 128), const),
        ],
        out_specs=[pl.BlockSpec((TM, 1024), row), pl.BlockSpec((TM, 512), row)],
        out_shape=[jax.ShapeDtypeStruct((n, 1024), BF16), jax.ShapeDtypeStruct((n, 512), BF16)],
        compiler_params=_cparams(("arbitrary",), 40 * 1024 * 1024),
        name="mla_ctx",
    )(ckv_ctx, kr_pad, wuk_pad, wuv, gkn)


def _attn_dense_kernel(q_ref, k_ref, v_ref, *rest, scale, has_ctx):
    if has_ctx:
        kc_ref, vc_ref, o_ref = rest
    else:
        (o_ref,) = rest
    v = v_ref[...]
    outs = []
    for hh in range(2):
        sl = slice(hh * 128, (hh + 1) * 128)
        q = q_ref[:, sl]
        s = _dot_nt(q, k_ref[:, sl]) * scale
        m = jnp.max(s, axis=-1, keepdims=True)
        if has_ctx:
            sc = _dot_nt(q, kc_ref[:, sl]) * scale
            m = jnp.maximum(m, jnp.max(sc, axis=-1, keepdims=True))
        p = jnp.exp(s - m)
        l = jnp.sum(p, axis=-1, keepdims=True)
        o = _dot(p.astype(BF16), v)
        if has_ctx:
            pc = jnp.exp(sc - m)
            l = l + jnp.sum(pc, axis=-1, keepdims=True)
            o = o + _dot(pc.astype(BF16), vc_ref[...])
        outs.append(o / l)
    lane = lax.broadcasted_iota(I32, outs[0].shape, 1)
    o_ref[...] = jnp.where(lane < 64, outs[0], outs[1]).astype(o_ref.dtype)


def mla_attn_prompt_call(qb, kb, vb):
    blk = lambda b, hp: (b, hp)
    return pl.pallas_call(
        functools.partial(_attn_dense_kernel, scale=128 ** -0.5, has_ctx=False),
        grid=(N_PROMPT_SEQ, 4),
        in_specs=[
            pl.BlockSpec((PROMPT_LEN, 256), blk),
            pl.BlockSpec((PROMPT_LEN, 256), blk),
            pl.BlockSpec((PROMPT_LEN, 128), blk),
        ],
        out_specs=pl.BlockSpec((PROMPT_LEN, 128), blk),
        out_shape=jax.ShapeDtypeStruct((T_P, 512), BF16),
        compiler_params=_cparams(("arbitrary", "arbitrary")),
        name="mla_attn_prompt",
    )(qb, kb, vb)


def mla_attn_sample_call(qb, kb, vb, kc, vc):
    tq = 256
    nq = SAMPLE_LEN // tq
    qoff = T_P // tq
    soff = T_P // SAMPLE_LEN
    return pl.pallas_call(
        functools.partial(_attn_dense_kernel, scale=128 ** -0.5, has_ctx=True),
        grid=(N_SAMPLE_SEQ, 4, nq),
        in_specs=[
            pl.BlockSpec((tq, 256), lambda b, hp, qi: (qoff + b * nq + qi, hp)),
            pl.BlockSpec((SAMPLE_LEN, 256), lambda b, hp, qi: (soff + b, hp)),
            pl.BlockSpec((SAMPLE_LEN, 128), lambda b, hp, qi: (soff + b, hp)),
            pl.BlockSpec((PAST, 256), lambda b, hp, qi: (b, hp)),
            pl.BlockSpec((PAST, 128), lambda b, hp, qi: (b, hp)),
        ],
        out_specs=pl.BlockSpec((tq, 128), lambda b, hp, qi: (b * nq + qi, hp)),
        out_shape=jax.ShapeDtypeStruct((T_S, 512), BF16),
        compiler_params=_cparams(("arbitrary", "arbitrary", "arbitrary"), 48 * 1024 * 1024),
        name="mla_attn_sample",
    )(qb, kb, vb, kc, vc)


def _group_mean_sq(x, width):
    gi = lax.broadcasted_iota(I32, (width, width), 0) // 64
    gj = lax.broadcasted_iota(I32, (width, width), 1) // 64
    bd = jnp.where(gi == gj, 1.0, 0.0).astype(BF16)
    hi, lo = _split_bf16(x * x)
    return (_dot(hi, bd) + _dot(lo, bd)) * (1.0 / 64.0)


def _dup64(x, lane):
    r = pltpu.roll(x, 64, 1)
    return jnp.concatenate([jnp.where(lane < 64, x, r), jnp.where(lane >= 64, x, r)], axis=-1)


def _odd_prep_kernel(q_ref, kv_ref, gq_ref, gk_ref, c_ref, sa_ref, sb_ref, q_out, kd_out, vd_out, k_out, v_out):
    ct, sat, sbt = c_ref[...], sa_ref[...], sb_ref[...]
    qd = q_ref[...]
    qn = qd * lax.rsqrt(_group_mean_sq(qd, 512) + EPS)
    for p in range(4):
        sl = slice(p * 128, (p + 1) * 128)
        q_out[:, sl] = _rope(qn[:, sl] * gq_ref[...], ct, sat, sbt).astype(q_out.dtype)
    kd = kv_ref[:, 0:128]
    vd = kv_ref[:, 128:256]
    kn = kd * lax.rsqrt(_group_mean_sq(kd, 128) + EPS) * gk_ref[...]
    k_out[...] = kn
    v_out[...] = vd
    lane = lax.broadcasted_iota(I32, kd.shape, 1)
    kd_out[...] = _dup64(_rope(kn, ct, sat, sbt), lane).astype(kd_out.dtype)
    vd_out[...] = _dup64(vd, lane).astype(vd_out.dtype)


def odd_prep_call(proj, gq2, gk2, tables):
    row = lambda i: (i, 0)
    const = lambda i: (0, 0)
    return pl.pallas_call(
        _odd_prep_kernel,
        grid=(NBLK,),
        in_specs=[
            pl.BlockSpec((TM, 512), lambda i: (i, 1)),
            pl.BlockSpec((TM, 256), lambda i: (i, 4)),
            pl.BlockSpec((1, 128), const),
            pl.BlockSpec((1, 128), const),
            pl.BlockSpec((TM, 128), row),
            pl.BlockSpec((TM, 128), row),
            pl.BlockSpec((TM, 128), row),
        ],
        out_specs=[
            pl.BlockSpec((TM, 512), row),
            pl.BlockSpec((TM, 256), row),
            pl.BlockSpec((TM, 256), row),
            pl.BlockSpec((TM, 128), row),
            pl.BlockSpec((TM, 128), row),
        ],
        out_shape=[
            jax.ShapeDtypeStruct((T, 512), BF16),
            jax.ShapeDtypeStruct((T, 256), BF16),
            jax.ShapeDtypeStruct((T, 256), BF16),
            jax.ShapeDtypeStruct((T, 128), F32),
            jax.ShapeDtypeStruct((T, 128), F32),
        ],
        compiler_params=_cparams(("arbitrary",)),
        name="odd_prep",
    )(proj, proj, gq2, gk2, *tables)


def _sink_heads(q_ref, sink_ref, o_ref, score_fn, value_fn, scale):
    lane = lax.broadcasted_iota(I32, (q_ref.shape[0], 128), 1)
    for p in range(4):
        qp = q_ref[:, p * 128:(p + 1) * 128]
        halves = []
        for par in range(2):
            h = 2 * p + par
            kv = h // 4
            qm = jnp.where((lane >= 64) == (par == 1), qp, jnp.zeros_like(qp))
            ss = score_fn(qm, kv)
            sk = sink_ref[h]
            m = jnp.maximum(functools.reduce(jnp.maximum, [jnp.max(s, axis=-1, keepdims=True) for s in ss]), sk)
            ps = [jnp.exp(s - m) for s in ss]
            l = functools.reduce(jnp.add, [jnp.sum(pp, axis=-1, keepdims=True) for pp in ps]) + jnp.exp(sk - m)
            halves.append(value_fn(ps, kv) / l)
        o_ref[:, p * 128:(p + 1) * 128] = jnp.where(lane < 64, halves[0], halves[1]).astype(o_ref.dtype)


def _win_prompt_kernel(sink_ref, q_ref, k_ref, v_ref, o_ref, *, scale):
    def score_fn(qm, kv):
        return [_dot_nt(qm, k_ref[:, kv * 128:(kv + 1) * 128]) * scale]

    def value_fn(ps, kv):
        return _dot(ps[0].astype(BF16), v_ref[:, kv * 128:(kv + 1) * 128])

    _sink_heads(q_ref, sink_ref, o_ref, score_fn, value_fn, scale)


def win_prompt_call(sink, q, kd, vd):
    blk = lambda b: (b, 0)
    return pl.pallas_call(
        functools.partial(_win_prompt_kernel, scale=64 ** -0.5),
        grid=(N_PROMPT_SEQ,),
        in_specs=[
            pl.BlockSpec(memory_space=pltpu.SMEM),
            pl.BlockSpec((PROMPT_LEN, 512), blk),
            pl.BlockSpec((PROMPT_LEN, 256), blk),
            pl.BlockSpec((PROMPT_LEN, 256), blk),
        ],
        out_specs=pl.BlockSpec((PROMPT_LEN, 512), blk),
        out_shape=jax.ShapeDtypeStruct((T_P, 512), BF16),
        compiler_params=_cparams(("arbitrary",)),
        name="win_prompt",
    )(sink, q, kd, vd)


WIN = 128


def _win_sample_kernel(sink_ref, q_ref, k0, k1, k2, v0, v1, v2, kc_ref, vc_ref, o_ref, *, scale):
    qi = pl.program_id(1)
    start = qi * WIN
    r = lax.broadcasted_iota(I32, (WIN, 3 * WIN), 0)
    c = lax.broadcasted_iota(I32, (WIN, 3 * WIN), 1)
    kpos = start - WIN + c
    valid = (c - r >= 0) & (c - r <= 2 * WIN) & (kpos >= 0) & (kpos < SAMPLE_LEN)

    def score_fn(qm, kv):
        sl = slice(kv * 128, (kv + 1) * 128)
        kloc = jnp.concatenate([k0[:, sl], k1[:, sl], k2[:, sl]], axis=0)
        s_loc = jnp.where(valid, _dot_nt(qm, kloc) * scale, NEG_INF)
        return [s_loc, _dot_nt(qm, kc_ref[:, sl]) * scale]

    def value_fn(ps, kv):
        sl = slice(kv * 128, (kv + 1) * 128)
        vloc = jnp.concatenate([v0[:, sl], v1[:, sl], v2[:, sl]], axis=0)
        return _dot(ps[0].astype(BF16), vloc) + _dot(ps[1].astype(BF16), vc_ref[:, sl])

    _sink_heads(q_ref, sink_ref, o_ref, score_fn, value_fn, scale)


def win_sample_call(sink, q, kpad, vpad, kc, vc):
    nq = SAMPLE_LEN // WIN
    qoff = T_P // WIN
    per = nq + 2
    loc = lambda d: (lambda b, qi: (b * per + qi + d, 0))
    return pl.pallas_call(
        functools.partial(_win_sample_kernel, scale=64 ** -0.5),
        grid=(N_SAMPLE_SEQ, nq),
        in_specs=[
            pl.BlockSpec(memory_space=pltpu.SMEM),
            pl.BlockSpec((WIN, 512), lambda b, qi: (qoff + b * nq + qi, 0)),
            pl.BlockSpec((WIN, 256), loc(0)),
            pl.BlockSpec((WIN, 256), loc(1)),
            pl.BlockSpec((WIN, 256), loc(2)),
            pl.BlockSpec((WIN, 256), loc(0)),
            pl.BlockSpec((WIN, 256), loc(1)),
            pl.BlockSpec((WIN, 256), loc(2)),
            pl.BlockSpec((PAST, 256), lambda b, qi: (b, 0)),
            pl.BlockSpec((PAST, 256), lambda b, qi: (b, 0)),
        ],
        out_specs=pl.BlockSpec((WIN, 512), lambda b, qi: (b * nq + qi, 0)),
        out_shape=jax.ShapeDtypeStruct((T_S, 512), BF16),
        compiler_params=_cparams(("arbitrary", "arbitrary")),
        name="win_sample",
    )(sink, q, kpad, kpad, kpad, vpad, vpad, vpad, kc, vc)


def _pool_kernel(xp_ref, xc_ref, xn_ref, w_ref, sc_ref, o_ref):
    i = pl.program_id(0)
    jj = (i - NBLK_P) % SBLK
    first = jnp.logical_or(i < NBLK_P, jj == 0)
    last = jnp.logical_or(i < NBLK_P, jj == SBLK - 1)
    ext = TM + 2 * POOL_HALO
    r = lax.broadcasted_iota(I32, (TM, ext), 0)
    c = lax.broadcasted_iota(I32, (TM, ext), 1) - POOL_HALO
    ok = jnp.logical_and(jnp.logical_or(c >= 0, jnp.logical_not(first)), jnp.logical_or(c < TM, jnp.logical_not(last)))
    rr = lax.broadcasted_iota(I32, (TM, 1), 0)
    parts = []
    for g, w in enumerate(POOL_WINDOWS):
        sl = slice(g * 128, (g + 1) * 128)
        x = xc_ref[:, sl]
        xe = jnp.concatenate([xp_ref[:, sl], x, xn_ref[:, sl]], axis=0)
        lo = r - w // 2
        band = (c >= lo) & (c < lo + w) & ok
        a = jnp.where(band, 1.0, 0.0).astype(BF16)
        lo1 = rr - w // 2
        lo_c = jnp.where(first, jnp.maximum(lo1, 0), lo1)
        hi_c = jnp.where(last, jnp.minimum(lo1 + w, TM), lo1 + w)
        cnt = (hi_c - lo_c).astype(F32)
        hi_x, lo_x = _split_bf16(xe)
        pooled = (_dot(a, hi_x) + _dot(a, lo_x)) / cnt
        y = _dot((pooled - x).astype(BF16), w_ref[g])
        parts.append(y)
    o_ref[...] = (jnp.concatenate(parts, axis=-1) * sc_ref[...]).astype(o_ref.dtype)


def pool_call(proj, w_pool, scale):
    hb = TM // POOL_HALO
    nh = T // POOL_HALO
    return pl.pallas_call(
        _pool_kernel,
        grid=(NBLK,),
        in_specs=[
            pl.BlockSpec((POOL_HALO, 512), lambda i: (jnp.maximum(i * hb - 1, 0), 0)),
            pl.BlockSpec((TM, 512), lambda i: (i, 0)),
            pl.BlockSpec((POOL_HALO, 512), lambda i: (jnp.minimum((i + 1) * hb, nh - 1), 0)),
            pl.BlockSpec((4, 128, 128), lambda i: (0, 0, 0)),
            pl.BlockSpec((1, 512), lambda i: (0, 0)),
        ],
        out_specs=pl.BlockSpec((TM, 512), lambda i: (i, 0)),
        out_shape=jax.ShapeDtypeStruct((T, 512), BF16),
        compiler_params=_cparams(("arbitrary",)),
        name="pool",
    )(proj, proj, proj, w_pool, scale)


TOPK = 16
TK_TM = 128


def _top16(s, key, aux=None):
    cols = s.shape[1]
    r16 = lax.broadcasted_iota(I32, (TOPK, cols), 0)
    vals = jnp.zeros((TOPK, cols), F32)
    outs = jnp.zeros((TOPK, cols), F32)
    for r in range(TOPK):
        m = jnp.max(s, axis=0, keepdims=True)
        p = jnp.min(jnp.where(s == m, key, np.float32(1e9)), axis=0, keepdims=True)
        hit = key == p
        o = p if aux is None else jnp.max(jnp.where(hit, aux, -1.0), axis=0, keepdims=True)
        vals = jnp.where(r16 == r, m, vals)
        outs = jnp.where(r16 == r, o, outs)
        s = jnp.where(hit, -jnp.inf, s)
    return vals, outs


def _pair_candidates(v1, i1, v2, i2):
    cols = v1.shape[1]
    n8 = lax.broadcasted_iota(I32, (8, cols), 0)
    n16 = lax.broadcasted_iota(I32, (16, cols), 0)
    f8, f16 = n8.astype(F32), n16.astype(F32)
    sc, pos, ex = [], [], []

    def add(score, position, expert, mask):
        sc.append(score if mask is None else jnp.where(mask, score, -jnp.inf))
        pos.append(position)
        ex.append(expert)

    add(v1[0:1] + v2, f16, i1[0:1] * 128.0 + i2, None)
    add(v1[1:2] + v2[0:8], 16.0 + f8, i1[1:2] * 128.0 + i2[0:8], None)
    add(v1[2:3] + v2[0:8], 32.0 + f8, i1[2:3] * 128.0 + i2[0:8], n8 < 5)
    add(v1[3:4] + v2[0:8], 48.0 + f8, i1[3:4] * 128.0 + i2[0:8], n8 < 4)
    add(v1 + v2[0:1], f16 * 16.0, i1 * 128.0 + i2[0:1], n16 >= 4)
    add(v1[0:8] + v2[1:2], f8 * 16.0 + 1.0, i1[0:8] * 128.0 + i2[1:2], n8 >= 4)
    add(v1[0:8] + v2[2:3], f8 * 16.0 + 2.0, i1[0:8] * 128.0 + i2[2:3], n8 == 4)
    return jnp.concatenate(sc, axis=0), jnp.concatenate(pos, axis=0), jnp.concatenate(ex, axis=0)


def _peer_topk_kernel(q_ref, sk_ref, idx_ref, gate_ref):
    rowf = lax.broadcasted_iota(I32, (128, q_ref.shape[0]), 0).astype(F32)
    idx_rows, gate_rows = [], []
    for h in range(8):
        s1 = _dot_nt(sk_ref[2 * h], q_ref[:, (2 * h) * 128:(2 * h + 1) * 128])
        s2 = _dot_nt(sk_ref[2 * h + 1], q_ref[:, (2 * h + 1) * 128:(2 * h + 2) * 128])
        v1, i1 = _top16(s1, rowf)
        v2, i2 = _top16(s2, rowf)
        cand, pos, expert = _pair_candidates(v1, i1, v2, i2)
        tv, te = _top16(cand, pos, expert)
        e = jnp.exp(tv - tv[0:1])
        gate_rows.append(e / jnp.sum(e, axis=0, keepdims=True))
        idx_rows.append(te)
    idx_ref[...] = (jnp.concatenate(idx_rows, axis=0).T * 4.0).astype(I32)
    gate_ref[...] = jnp.concatenate(gate_rows, axis=0).T


def peer_topk_call(q, sk):
    row = lambda i: (i, 0)
    return pl.pallas_call(
        _peer_topk_kernel,
        grid=(T // TK_TM,),
        in_specs=[pl.BlockSpec((TK_TM, 2048), row), pl.BlockSpec((16, 128, 128), lambda i: (0, 0, 0))],
        out_specs=[pl.BlockSpec((TK_TM, NPICK), row), pl.BlockSpec((TK_TM, NPICK), row)],
        out_shape=[jax.ShapeDtypeStruct((T, NPICK), I32), jax.ShapeDtypeStruct((T, NPICK), F32)],
        compiler_params=_cparams(("arbitrary",)),
        name="peer_topk",
    )(q, sk)


def _gather_token(idx_smem, tab_ref, stage_ref, g, j):
    for k in range(NPICK):
        e4 = pl.multiple_of(idx_smem[g * PEER_TG + j, k], 4)
        stage_ref[pl.ds(j * 512 + 4 * k, 4), :] = tab_ref[pl.ds(e4, 4), :]


def _idx_copy(idx_hbm, idx_smem, sems, block, g):
    row0 = block * PEER_TB + g * PEER_TG
    return pltpu.make_async_copy(
        idx_hbm.at[pl.ds(row0, PEER_TG)], idx_smem.at[pl.ds(g * PEER_TG, PEER_TG)], sems.at[g])


def _gather_schedule(idx_hbm, idx_smem, sems, tab_ref, stage, begin, token, end):
    ngroup = PEER_TB // PEER_TG
    i = pl.program_id(0)
    last = pl.num_programs(0) - 1
    nxt = jnp.minimum(i + 1, last)
    nxt2 = jnp.minimum(i + 2, last)

    @pl.when(i == 0)
    def _():
        for g in range(ngroup):
            _idx_copy(idx_hbm, idx_smem, sems, 0, g).start()
        _idx_copy(idx_hbm, idx_smem, sems, 0, 0).wait()
        for j in range(PEER_TG):
            _gather_token(idx_smem, tab_ref, stage, 0, j)
        _idx_copy(idx_hbm, idx_smem, sems, nxt, 0).start()

    pending = None
    for g in range(ngroup):
        base = g * PEER_TG
        gn = (g + 1) % ngroup
        state = begin(base)
        _idx_copy(idx_hbm, idx_smem, sems, 0, gn).wait()
        for j in range(PEER_TG):
            state = token(state, stage, base, j)
            _gather_token(idx_smem, tab_ref, stage, gn, j)
            if j == 1 and pending is not None:
                end(*pending)
        _idx_copy(idx_hbm, idx_smem, sems, nxt if gn else nxt2, gn).start()
        pending = (state, base)
    end(*pending)

    @pl.when(i == last)
    def _():
        for g in range(ngroup):
            _idx_copy(idx_hbm, idx_smem, sems, 0, g).wait()


def _staged_tiles(stage_ref, j):
    return pltpu.bitcast(stage_ref[pl.ds(j * 512, 512), :], BF16)


def _peer_u_kernel(idx_hbm, x_ref, gate_ref, tab_ref, act_ref, idx_smem, stage, sems):
    col = lax.broadcasted_iota(I32, (16, 1024), 1)
    row = lax.broadcasted_iota(I32, (16, 1024), 0)
    diag = (col % 8) == (row % 8)
    gi = lax.broadcasted_iota(I32, (1024, 128), 0)
    gj = lax.broadcasted_iota(I32, (1024, 128), 1)
    gsum = jnp.where(gi // 8 == gj, 1.0, 0.0).astype(BF16)
    r8 = lax.broadcasted_iota(I32, (8, 1024), 0)

    def begin(base):
        return jnp.zeros((8, 1024), F32)

    def token(zs, stage_ref, base, j):
        xh, xl = _split_bf16(x_ref[base + j])
        lhs = jnp.concatenate([xh, xl], axis=0)
        z = _dot_nt(lhs, _staged_tiles(stage_ref, j))
        zr = jnp.sum(jnp.where(diag, z, 0.0), axis=0, keepdims=True)
        return jnp.where(r8 == j, zr, zs)

    def end(zs, base):
        zh, zl = _split_bf16(zs)
        sc = _dot(jnp.concatenate([zh, zl], axis=0), gsum)
        sc = sc[0:8] + sc[8:16]
        gate = gate_ref[pl.ds(base, PEER_TG), :]
        act = 0.5 * sc * (1.0 + lax.erf(sc * np.float32(1.0 / np.sqrt(2.0)))) * gate
        act_ref[pl.ds(base, PEER_TG), :] = act

    _gather_schedule(idx_hbm, idx_smem, sems, tab_ref.at[0], stage, begin, token, end)


def _peer_v_kernel(idx_hbm, act_ref, tab_ref, out_ref, idx_smem, stage, sems):
    ei = lax.broadcasted_iota(I32, (128, 1024), 0)
    ej = lax.broadcasted_iota(I32, (128, 1024), 1)
    expand = jnp.where(ej // 8 == ei, 1.0, 0.0).astype(BF16)
    col = lax.broadcasted_iota(I32, (8, 1024), 1)
    row = lax.broadcasted_iota(I32, (8, 1024), 0)
    diag = (col % 8) == row

    def begin(base):
        ah, al = _split_bf16(act_ref[pl.ds(base, PEER_TG), :])
        rep = _dot(jnp.concatenate([ah, al], axis=0), expand)
        return rep[0:8], rep[8:16]

    def token(rep, stage_ref, base, j):
        lh = jnp.where(diag, rep[0][j:j + 1, :], 0.0).astype(BF16)
        ll = jnp.where(diag, rep[1][j:j + 1, :], 0.0).astype(BF16)
        o = _dot(jnp.concatenate([lh, ll], axis=0), _staged_tiles(stage_ref, j))
        out_ref[base + j] = o[0:8] + o[8:16]
        return rep

    def end(rep, base):
        pass

    _gather_schedule(idx_hbm, idx_smem, sems, tab_ref.at[0], stage, begin, token, end)


def _gather_scratch():
    return [
        pltpu.SMEM((PEER_TB, NPICK), I32),
        pltpu.VMEM((PEER_TG * 512, 128), jnp.uint32),
        pltpu.SemaphoreType.DMA((PEER_TB // PEER_TG,)),
    ]


PACK_BE = 512


def _pack_kernel(t_ref, o_ref):
    x = t_ref[0]
    for s in range(4):
        lo = x[:, 256 * s:256 * s + 128].astype(BF16).astype(F32)
        hi = x[:, 256 * s + 128:256 * s + 256].astype(BF16).astype(F32)
        word = (pltpu.bitcast(lo, jnp.uint32) >> 16) | (pltpu.bitcast(hi, jnp.uint32) & jnp.uint32(0xFFFF0000))
        o_ref[0, pl.ds(s, PACK_BE, stride=4), :] = word


def pack_tables_call(tabs):
    depth = tabs.shape[0]
    return pl.pallas_call(
        _pack_kernel,
        grid=(depth, P_EXPERTS // PACK_BE),
        in_specs=[pl.BlockSpec((1, PACK_BE, D), lambda l, i: (l, i, 0))],
        out_specs=pl.BlockSpec((1, PACK_BE * 4, 128), lambda l, i: (l, i, 0)),
        out_shape=jax.ShapeDtypeStruct((depth, P_EXPERTS * 4, 128), jnp.uint32),
        compiler_params=_cparams(("arbitrary", "arbitrary")),
        name="pack_tables",
    )(tabs)


def peer_u_call(idx, x3, gate, tabs, layer):
    tb = PEER_TB
    return pl.pallas_call(
        _peer_u_kernel,
        grid=(T // tb,),
        in_specs=[
            pl.BlockSpec(memory_space=pl.ANY),
            pl.BlockSpec((tb, 8, 128), lambda i: (i, 0, 0)),
            pl.BlockSpec((tb, NPICK), lambda i: (i, 0)),
            pl.BlockSpec((1, P_EXPERTS * 4, 128), lambda i: (layer, 0, 0), pipeline_mode=pl.Buffered(1)),
        ],
        out_specs=pl.BlockSpec((tb, NPICK), lambda i: (i, 0)),
        out_shape=jax.ShapeDtypeStruct((T, NPICK), F32),
        scratch_shapes=_gather_scratch(),
        compiler_params=_cparams(("arbitrary",), VMEM_LIMIT),
        name="peer_u",
    )(idx, x3, gate, tabs)


def peer_v_call(idx, act, tabs, layer):
    tb = PEER_TB
    return pl.pallas_call(
        _peer_v_kernel,
        grid=(T // tb,),
        in_specs=[
            pl.BlockSpec(memory_space=pl.ANY),
            pl.BlockSpec((tb, NPICK), lambda i: (i, 0)),
            pl.BlockSpec((1, P_EXPERTS * 4, 128), lambda i: (layer, 0, 0), pipeline_mode=pl.Buffered(1)),
        ],
        out_specs=pl.BlockSpec((tb, 8, 128), lambda i: (i, 0, 0)),
        out_shape=jax.ShapeDtypeStruct((T, 8, 128), F32),
        scratch_shapes=_gather_scratch(),
        compiler_params=_cparams(("arbitrary",), VMEM_LIMIT),
        name="peer_v",
    )(idx, act, tabs)


def _even_in_weight(w):
    qa, ka, va, ra, glf, glb, cq, ckv, kr = jnp.split(w, np.cumsum([256, 256, 512, 512, 16, 16, 256, 256, 64])[:-1].tolist(), axis=-1)
    pad = jnp.zeros((D, 32), w.dtype)
    return jnp.concatenate([qa, ka, va, ra, cq, ckv, glf, glb, pad, kr], axis=-1).astype(BF16)


def _gate_up_pad(w_gu):
    z = jnp.zeros((2, 128, 256), F32)
    z = z.at[0, 0:16].set(w_gu[0])
    z = z.at[1, 16:32].set(w_gu[1])
    return z.astype(BF16)


def _uk_pad(w_uk):
    w = w_uk.reshape(256, 8, 64)
    return jnp.concatenate([w, jnp.zeros_like(w)], axis=-1).reshape(256, 1024).astype(BF16)


def _dup_heads(x):
    a, b = x[:, :64], x[:, 64:]
    return jnp.concatenate([a, a, b, b], axis=-1).astype(BF16)


def _pad_latent(x):
    x = x.reshape(N_SAMPLE_SEQ, SAMPLE_LEN, 256)
    x = jnp.pad(x, ((0, 0), (WIN, WIN), (0, 0)))
    return x.reshape(N_SAMPLE_SEQ * (SAMPLE_LEN + 2 * WIN), 256)


def kernel(x_prompt, x_sample, state_gla, cache_mla_ckv, cache_mla_krope, cache_win_kv, c, c_ctx, g_norm, w_ada, b_ada, w_in_even, w_gate_up, b_gate_up, g_gla_out, g_mla_cq, g_mla_ckv, w_mla_uq, w_mla_uk, w_mla_uv, g_mla_qn, g_mla_kn, w_out_even, w_in_odd, w_pool, pool_scale, g_win_qn, g_win_kn, win_sink, w_out_odd, peer_wq, peer_subkeys, peer_u, peer_v):
    depth = w_ada.shape[0]
    x = jnp.concatenate([x_prompt.reshape(T_P, D), x_sample.reshape(T_S, D)], axis=0)
    cond8 = jnp.concatenate([c_ctx[None], c, jnp.zeros((3, D), F32)], axis=0)
    mods_all = ada_call(cond8, w_ada, b_ada).reshape(depth, 8, 6, D)
    tables_b = _rope_tables("upper")
    tables_d = _rope_tables("all")
    tabs_u = pack_tables_call(peer_u)
    tabs_v = pack_tables_call(peer_v)

    gla_states, mla_ckv, mla_kr, win_kv = [], [], [], []
    for l in range(depth):
        mods = mods_all[l]
        if l % 2 == 0:
            e = l // 2
            (proj,) = modproj_call(x, g_norm[l, 0], mods, _even_in_weight(w_in_even[e]), 0, F32, False)
            s0 = jnp.concatenate([jnp.zeros((N_PROMPT_SEQ, 2, 4, 64, 128), F32), state_gla[:, e]], axis=0)
            s0t = jnp.swapaxes(s0.reshape(-1, 2, 256, 128), -1, -2)
            o_f, o_b, sfin = gla_call(proj, _gate_up_pad(w_gate_up[e]), b_gate_up[e].reshape(2, 1, 256), s0t)
            sfin = jnp.swapaxes(sfin[:N_PROMPT_SEQ], -1, -2).reshape(N_PROMPT_SEQ, 2, 4, 64, 128)
            gla_states.append(sfin)
            wuk_pad = _uk_pad(w_mla_uk[e])
            wuv = w_mla_uv[e].astype(BF16)
            gkn = g_mla_kn[e].reshape(1, 128)
            qb, kb, vb, ckv_n, krf = mla_prep_call(
                proj, g_mla_cq[e].reshape(1, 256), g_mla_ckv[e].reshape(1, 256), w_mla_uq[e].astype(BF16),
                wuk_pad, wuv, g_mla_qn[e].reshape(1, 128), gkn, tables_b)
            mla_ckv.append(ckv_n[:T_P].reshape(N_PROMPT_SEQ, PROMPT_LEN, 256))
            mla_kr.append(krf[:T_P, 64:].reshape(N_PROMPT_SEQ, PROMPT_LEN, 64))
            kr_ctx = cache_mla_krope[:, e].reshape(N_SAMPLE_SEQ * PAST, 64)
            kr_pad = jnp.concatenate([jnp.zeros_like(kr_ctx), kr_ctx], axis=-1)
            kc, vc = mla_ctx_call(cache_mla_ckv[:, e].reshape(N_SAMPLE_SEQ * PAST, 256), kr_pad, wuk_pad, wuv, gkn)
            ob = jnp.concatenate([mla_attn_prompt_call(qb, kb, vb), mla_attn_sample_call(qb, kb, vb, kc, vc)], axis=0)
            x = outproj_call(x, o_f, o_b, proj, 2, g_gla_out[e].reshape(1, 128), ob, w_out_even[e].astype(BF16), mods, True)
        else:
            o = l // 2
            (proj,) = modproj_call(x, g_norm[l, 0], mods, w_in_odd[o].astype(BF16), 0, F32, False)
            oc = pool_call(proj, w_pool[o].astype(BF16), pool_scale[o].reshape(1, 512))
            gq2 = jnp.tile(g_win_qn[o], 2).reshape(1, 128)
            gk2 = jnp.tile(g_win_kn[o], 2).reshape(1, 128)
            qw, kd, vd, kn, vn = odd_prep_call(proj, gq2, gk2, tables_d)
            k_p = kn[:T_P].reshape(N_PROMPT_SEQ, PROMPT_LEN, 2, 64)
            v_p = vn[:T_P].reshape(N_PROMPT_SEQ, PROMPT_LEN, 2, 64)
            win_kv.append(jnp.stack([k_p, v_p], axis=1))
            kc = _dup_heads(cache_win_kv[:, o, 0].reshape(N_SAMPLE_SEQ * PAST, 128))
            vc = _dup_heads(cache_win_kv[:, o, 1].reshape(N_SAMPLE_SEQ * PAST, 128))
            od = jnp.concatenate([
                win_prompt_call(win_sink[o], qw, kd, vd),
                win_sample_call(win_sink[o], qw, _pad_latent(kd[T_P:]), _pad_latent(vd[T_P:]), kc, vc)], axis=0)
            x = outproj_call(x, oc, oc, proj, 0, jnp.ones((1, 128), F32), od, w_out_odd[o].astype(BF16), mods, False)
        q, h = modproj_call(x, g_norm[l, 1], mods, peer_wq[l].astype(BF16), 3, BF16, True)
        idx, gate = peer_topk_call(q, peer_subkeys[l].reshape(16, 128, 128).astype(BF16))
        act = peer_u_call(idx, h.reshape(T, 8, 128), gate, tabs_u, l)
        y = peer_v_call(idx, act, tabs_v, l).reshape(T, D)
        x = resid_call(x, y, mods, 5)

    return (
        x[:T_P].reshape(N_PROMPT_SEQ, PROMPT_LEN, D),
        x[T_P:].reshape(N_SAMPLE_SEQ, SAMPLE_LEN, D),
        jnp.stack(gla_states, axis=1),
        jnp.stack(mla_ckv, axis=1),
        jnp.stack(mla_kr, axis=1),
        jnp.stack(win_kv, axis=1),
    )
```

```python
import functools

import jax
import jax.numpy as jnp
import numpy as np
from jax import lax
from jax.experimental import pallas as pl
from jax.experimental.pallas import tpu as pltpu

F32 = jnp.float32
BF16 = jnp.bfloat16
I32 = jnp.int32

D = 1024
N_PROMPT_SEQ = 32
PROMPT_LEN = 256
N_SAMPLE_SEQ = 4
SAMPLE_LEN = 4096
PAST = 512
T_P = N_PROMPT_SEQ * PROMPT_LEN
T_S = N_SAMPLE_SEQ * SAMPLE_LEN
T = T_P + T_S
GRID_W = 64
EPS = 1e-6
NEG_INF = -1e30
ROPE_BASE = 10000.0

TM = 256
NBLK = T // TM
NBLK_P = T_P // TM
SBLK = SAMPLE_LEN // TM

EVEN_W = 2176
ODD_W = 1280
POOL_WINDOWS = (2, 4, 8, 16)
POOL_HALO = 8

P_EXPERTS = 16384
NPICK = 128
PEER_TB = 64
PEER_TG = 8
VMEM_LIMIT = 56 * 1024 * 1024


def _cparams(sem, vmem=None):
    return pltpu.CompilerParams(dimension_semantics=sem, vmem_limit_bytes=vmem)


def _mod_group(i):
    return jnp.where(i < NBLK_P, 0, 1 + (i - NBLK_P) // SBLK)


def _split_bf16(x):
    hi = x.astype(BF16)
    lo = (x - hi.astype(F32)).astype(BF16)
    return hi, lo


def _dot(a, b):
    return jnp.dot(a, b, preferred_element_type=F32)


def _dot_nt(a, b):
    return lax.dot_general(a, b, (((1,), (1,)), ((), ())), preferred_element_type=F32)


def _rms(x, g):
    ms = jnp.mean(x * x, axis=-1, keepdims=True)
    return x * lax.rsqrt(ms + EPS) * g


def _ada_kernel(c_ref, w_ref, b_ref, o_ref):
    c = c_ref[...]
    s = c / (1.0 + jnp.exp(-c))
    o_ref[0] = _dot(s.astype(BF16), w_ref[0].astype(BF16)) + b_ref[0]


def ada_call(cond8, w_ada, b_ada):
    depth, _, n6 = w_ada.shape
    tn = 1536
    return pl.pallas_call(
        _ada_kernel,
        grid=(depth, n6 // tn),
        in_specs=[
            pl.BlockSpec((8, D), lambda l, j: (0, 0)),
            pl.BlockSpec((1, D, tn), lambda l, j: (l, 0, j)),
            pl.BlockSpec((1, 1, tn), lambda l, j: (l, 0, j)),
        ],
        out_specs=pl.BlockSpec((1, 8, tn), lambda l, j: (l, 0, j)),
        out_shape=jax.ShapeDtypeStruct((depth, 8, n6), F32),
        compiler_params=_cparams(("arbitrary", "arbitrary"), 40 * 1024 * 1024),
        name="ada",
    )(cond8, w_ada, b_ada.reshape(depth, 1, n6))


def _modproj_kernel(*refs, a, with_h, fused):
    it = iter(refs)
    x_ref = next(it)
    y_ref, prev_ref = (next(it), next(it)) if fused else (None, None)
    g_ref, mod_ref, w_ref, o_ref = next(it), next(it), next(it), next(it)
    h_ref = next(it) if with_h else None
    x = x_ref[...]
    if fused:
        x = x + prev_ref[0][5:6] * y_ref[...]
        next(it)[...] = x
    m = mod_ref[0]
    h = _rms(x, g_ref[...]) * (1.0 + m[a + 1:a + 2]) + m[a:a + 1]
    o_ref[...] = _dot(h.astype(BF16), w_ref[...]).astype(o_ref.dtype)
    if with_h:
        h_ref[...] = h


def modproj_call(x, g, mods, w, a, out_dtype, with_h, y=None, mods_prev=None):
    n = w.shape[1]
    row = lambda i: (i, 0)
    grp = lambda i: (_mod_group(i), 0, 0)
    fused = y is not None
    out_shape = [jax.ShapeDtypeStruct((T, n), out_dtype)]
    out_specs = [pl.BlockSpec((TM, n), row)]
    for _ in range(int(with_h) + int(fused)):
        out_shape.append(jax.ShapeDtypeStruct((T, D), F32))
        out_specs.append(pl.BlockSpec((TM, D), row))
    in_specs = [pl.BlockSpec((TM, D), row)]
    args = [x]
    if fused:
        in_specs += [pl.BlockSpec((TM, D), row), pl.BlockSpec((1, 6, D), grp)]
        args += [y, mods_prev]
    in_specs += [pl.BlockSpec((1, D), lambda i: (0, 0)), pl.BlockSpec((1, 6, D), grp), pl.BlockSpec((D, n), lambda i: (0, 0))]
    args += [g.reshape(1, D), mods, w]
    return pl.pallas_call(
        functools.partial(_modproj_kernel, a=a, with_h=with_h, fused=fused),
        grid=(NBLK,),
        in_specs=in_specs,
        out_specs=out_specs,
        out_shape=out_shape,
        compiler_params=_cparams(("arbitrary",), 40 * 1024 * 1024),
        name="modproj",
    )(*args)


def _resid_kernel(x_ref, y_ref, mod_ref, o_ref, *, a):
    o_ref[...] = x_ref[...] + mod_ref[0][a:a + 1] * y_ref[...]


def resid_call(x, y, mods, a):
    return pl.pallas_call(
        functools.partial(_resid_kernel, a=a),
        grid=(NBLK,),
        in_specs=[
            pl.BlockSpec((TM, D), lambda i: (i, 0)),
            pl.BlockSpec((TM, D), lambda i: (i, 0)),
            pl.BlockSpec((1, 6, D), lambda i: (_mod_group(i), 0, 0)),
        ],
        out_specs=pl.BlockSpec((TM, D), lambda i: (i, 0)),
        out_shape=jax.ShapeDtypeStruct((T, D), F32),
        compiler_params=_cparams(("arbitrary",)),
        name="resid",
    )(x, y, mods)


def _outproj_kernel(x_ref, a_ref, a2_ref, r_ref, ggo_ref, b_ref, w_ref, mod_ref, o_ref, *, gla_post):
    if gla_post:
        o = a_ref[...] + a2_ref[...]
        r = r_ref[...]
        parts = []
        for h in range(4):
            seg = _rms(o[:, h * 128:(h + 1) * 128], ggo_ref[...])
            rr = r[:, h * 128:(h + 1) * 128]
            parts.append(seg * (rr / (1.0 + jnp.exp(-rr))))
        mix_a = jnp.concatenate(parts, axis=-1)
    else:
        mix_a = a_ref[...]
    w = w_ref[...]
    y = _dot(mix_a.astype(BF16), w[0:512]) + _dot(b_ref[...].astype(BF16), w[512:1024])
    o_ref[...] = x_ref[...] + mod_ref[0][2:3] * y


def outproj_call(x, mix_a, mix_a2, proj, ra_col, g_go, mix_b, w, mods, gla_post):
    row = lambda i: (i, 0)
    return pl.pallas_call(
        functools.partial(_outproj_kernel, gla_post=gla_post),
        grid=(NBLK,),
        in_specs=[
            pl.BlockSpec((TM, D), row),
            pl.BlockSpec((TM, 512), row),
            pl.BlockSpec((TM, 512), row),
            pl.BlockSpec((TM, 512), lambda i: (i, ra_col)),
            pl.BlockSpec((1, 128), lambda i: (0, 0)),
            pl.BlockSpec((TM, 512), row),
            pl.BlockSpec((D, D), lambda i: (0, 0)),
            pl.BlockSpec((1, 6, D), lambda i: (_mod_group(i), 0, 0)),
        ],
        out_specs=pl.BlockSpec((TM, D), row),
        out_shape=jax.ShapeDtypeStruct((T, D), F32),
        compiler_params=_cparams(("arbitrary",), 40 * 1024 * 1024),
        name="outproj",
    )(x, mix_a, mix_a2, proj, g_go, mix_b, w, mods)


GLA_CHUNK = 64


def _log_sigmoid(z):
    return jnp.minimum(z, 0.0) - jnp.log1p(jnp.exp(-jnp.abs(z)))


def _gla_direction(qk_ref, v_ref, gl_ref, wgu, bgu, st_ref, o_ref, reverse):
    ri = lax.broadcasted_iota(I32, (GLA_CHUNK, GLA_CHUNK), 0)
    ci = lax.broadcasted_iota(I32, (GLA_CHUNK, GLA_CHUNK), 1)
    keep = (ci >= ri) if reverse else (ci <= ri)
    tri = jnp.where(keep, 1.0, 0.0).astype(BF16)
    lane = lax.broadcasted_iota(I32, (GLA_CHUNK, 256), 1)
    lane_s = lax.broadcasted_iota(I32, (128, 256), 1)
    nchunk = TM // GLA_CHUNK
    order = range(nchunk - 1, -1, -1) if reverse else range(nchunk)
    for c in order:
        r0 = c * GLA_CHUNK
        q = qk_ref[r0:r0 + GLA_CHUNK, 0:256] * 0.125
        k = qk_ref[r0:r0 + GLA_CHUNK, 256:512]
        v = v_ref[r0:r0 + GLA_CHUNK, :]
        z = _dot(gl_ref[r0:r0 + GLA_CHUNK, :].astype(BF16), wgu) + bgu
        la = _log_sigmoid(z) * (1.0 / 16.0)
        la_hi, la_lo = _split_bf16(la)
        b = _dot(tri, la_hi) + _dot(tri, la_lo)
        b_tot = b[0:1, :] if reverse else b[GLA_CHUNK - 1:GLA_CHUNK, :]
        q_dec = q * jnp.exp(b)
        k_intra = (k * jnp.exp(-b)).astype(BF16)
        k_state = (k * jnp.exp(b_tot - b)).astype(BF16)
        st = st_ref[...]
        st_b = st.astype(BF16)
        v_b = v.astype(BF16)
        vt = v.T.astype(BF16)
        outs = []
        upd = jnp.zeros((128, 256), F32)
        for h in range(4):
            qm = jnp.where(lane // 64 == h, q_dec, 0.0).astype(BF16)
            att = jnp.where(keep, _dot_nt(qm, k_intra), 0.0)
            o_h = _dot(att.astype(BF16), v_b[:, h * 128:(h + 1) * 128]) + _dot_nt(qm, st_b)
            outs.append(o_h)
            u_h = _dot(vt[h * 128:(h + 1) * 128, :], k_state)
            upd = jnp.where(lane_s // 64 == h, u_h, upd)
        st_ref[...] = st * jnp.exp(b_tot) + upd
        o_ref[r0:r0 + GLA_CHUNK, :] = jnp.concatenate(outs, axis=-1)


def _gla_kernel(qkf, vf, glf, qkb, vb, glb, wgu_ref, bgu_ref, s0_ref, of_ref, ob_ref, sfin_ref, sf_scr, sb_scr):
    i = pl.program_id(0)
    jj = (i - NBLK_P) % SBLK
    first = jnp.logical_or(i < NBLK_P, jj == 0)
    last = jnp.logical_or(i < NBLK_P, jj == SBLK - 1)

    @pl.when(first)
    def _():
        sf_scr[...] = s0_ref[0, 0]
        sb_scr[...] = s0_ref[0, 1]

    _gla_direction(qkf, vf, glf, wgu_ref[0], bgu_ref[0], sf_scr, of_ref, False)
    _gla_direction(qkb, vb, glb, wgu_ref[1], bgu_ref[1], sb_scr, ob_ref, True)

    @pl.when(last)
    def _():
        sfin_ref[0, 0] = sf_scr[...]
        sfin_ref[0, 1] = sb_scr[...]


def _seq_of_block(i):
    return jnp.where(i < NBLK_P, i, NBLK_P + (i - NBLK_P) // SBLK)


def _rev_block(i):
    s = (i - NBLK_P) // SBLK
    jj = (i - NBLK_P) % SBLK
    return jnp.where(i < NBLK_P, i, NBLK_P + s * SBLK + (SBLK - 1 - jj))


def gla_call(proj, wgu_pad, bgu, s0t):
    nseq = N_PROMPT_SEQ + N_SAMPLE_SEQ
    fwd = lambda c: (lambda i: (i, c))
    bwd = lambda c: (lambda i: (_rev_block(i), c))
    return pl.pallas_call(
        _gla_kernel,
        grid=(NBLK,),
        in_specs=[
            pl.BlockSpec((TM, 512), fwd(0)),
            pl.BlockSpec((TM, 512), fwd(1)),
            pl.BlockSpec((TM, 128), fwd(16)),
            pl.BlockSpec((TM, 512), bwd(0)),
            pl.BlockSpec((TM, 512), bwd(1)),
            pl.BlockSpec((TM, 128), bwd(16)),
            pl.BlockSpec((2, 128, 256), lambda i: (0, 0, 0)),
            pl.BlockSpec((2, 1, 256), lambda i: (0, 0, 0)),
            pl.BlockSpec((1, 2, 128, 256), lambda i: (_seq_of_block(i), 0, 0, 0)),
        ],
        out_specs=[
            pl.BlockSpec((TM, 512), fwd(0)),
            pl.BlockSpec((TM, 512), bwd(0)),
            pl.BlockSpec((1, 2, 128, 256), lambda i: (_seq_of_block(i), 0, 0, 0)),
        ],
        out_shape=[
            jax.ShapeDtypeStruct((T, 512), F32),
            jax.ShapeDtypeStruct((T, 512), F32),
            jax.ShapeDtypeStruct((nseq, 2, 128, 256), F32),
        ],
        scratch_shapes=[pltpu.VMEM((128, 256), F32), pltpu.VMEM((128, 256), F32)],
        compiler_params=_cparams(("arbitrary",), 40 * 1024 * 1024),
        name="gla",
    )(proj, proj, proj, proj, proj, proj, wgu_pad, bgu, s0t)


def _rope(x, c, sa, sb):
    return x * c + pltpu.roll(x, 112, 1) * sa + pltpu.roll(x, 16, 1) * sb


def _rope_tables(rot_lanes):
    rows = SAMPLE_LEN // GRID_W
    quarter = 16
    inv = ROPE_BASE ** (-jnp.arange(quarter, dtype=F32) / quarter)
    row = jnp.repeat(jnp.arange(rows, dtype=F32), GRID_W)
    col = jnp.tile(jnp.arange(GRID_W, dtype=F32), rows)
    ar = row[:, None] * inv
    ac = col[:, None] * inv
    ang = jnp.concatenate([ar, ar, ac, ac], axis=-1)
    cos, sin = jnp.cos(ang), jnp.sin(ang)
    seg = (np.arange(64) // 16) % 2
    sa64 = jnp.where(seg == 0, -sin, 0.0)
    sb64 = jnp.where(seg == 1, sin, 0.0)
    if rot_lanes == "upper":
        c = jnp.concatenate([jnp.ones_like(cos), cos], axis=-1)
        sa = jnp.concatenate([jnp.zeros_like(sin), sa64], axis=-1)
        sb = jnp.concatenate([jnp.zeros_like(sin), sb64], axis=-1)
    else:
        c = jnp.concatenate([cos, cos], axis=-1)
        sa = jnp.concatenate([sa64, sa64], axis=-1)
        sb = jnp.concatenate([sb64, sb64], axis=-1)

    def full(t, fill):
        t = jnp.tile(t, (N_SAMPLE_SEQ, 1))
        return jnp.concatenate([jnp.full((T_P, 128), fill, F32), t], axis=0)

    return full(c, 1.0), full(sa, 0.0), full(sb, 0.0)


def _mla_prep_kernel(*refs, with_q, norm_ckv, rope):
    it = iter(refs)
    cq_ref = next(it) if with_q else None
    ckv_ref = next(it)
    kr_ref = next(it)
    gcq_ref = next(it) if with_q else None
    gckv_ref = next(it) if norm_ckv else None
    wuq_ref = next(it) if with_q else None
    wuk_ref = next(it)
    wuv_ref = next(it)
    gqn_ref = next(it) if with_q else None
    gkn_ref = next(it)
    if rope:
        c_ref, sa_ref, sb_ref = next(it), next(it), next(it)
    q_out = next(it) if with_q else None
    k_out = next(it)
    v_out = next(it)
    ckv_out = next(it) if norm_ckv else None
    kr_out = next(it) if norm_ckv else None

    if rope:
        ct, sat, sbt = c_ref[...], sa_ref[...], sb_ref[...]
    ckv = ckv_ref[...]
    if norm_ckv:
        ckv = _rms(ckv, gckv_ref[...])
        ckv_out[...] = ckv
    lane = lax.broadcasted_iota(I32, kr_ref.shape, 1)
    krf = jnp.where(lane >= 64, kr_ref[...], 0.0)
    if norm_ckv:
        kr_out[...] = krf
    ckv_b = ckv.astype(BF16)
    knope = _dot(ckv_b, wuk_ref[...])
    v_out[...] = _dot(ckv_b, wuv_ref[...]).astype(v_out.dtype)
    if with_q:
        qb = _dot(_rms(cq_ref[...], gcq_ref[...]).astype(BF16), wuq_ref[...])
    for h in range(8):
        sl = slice(h * 128, (h + 1) * 128)
        kh = _rms(knope[:, sl] + krf, gkn_ref[...])
        if rope:
            kh = _rope(kh, ct, sat, sbt)
        k_out[:, sl] = kh.astype(k_out.dtype)
        if with_q:
            qh = _rms(qb[:, sl], gqn_ref[...])
            if rope:
                qh = _rope(qh, ct, sat, sbt)
            q_out[:, sl] = qh.astype(q_out.dtype)


def mla_prep_call(proj, gcq, gckv, wuq, wuk_pad, wuv, gqn, gkn, tables):
    row = lambda i: (i, 0)
    const = lambda i: (0, 0)
    return pl.pallas_call(
        functools.partial(_mla_prep_kernel, with_q=True, norm_ckv=True, rope=True),
        grid=(NBLK,),
        in_specs=[
            pl.BlockSpec((TM, 256), lambda i: (i, 6)),
            pl.BlockSpec((TM, 256), lambda i: (i, 7)),
            pl.BlockSpec((TM, 128), lambda i: (i, 16)),
            pl.BlockSpec((1, 256), const),
            pl.BlockSpec((1, 256), const),
            pl.BlockSpec((256, 1024), const),
            pl.BlockSpec((256, 1024), const),
            pl.BlockSpec((256, 512), const),
            pl.BlockSpec((1, 128), const),
            pl.BlockSpec((1, 128), const),
            pl.BlockSpec((TM, 128), row),
            pl.BlockSpec((TM, 128), row),
            pl.BlockSpec((TM, 128), row),
        ],
        out_specs=[
            pl.BlockSpec((TM, 1024), row),
            pl.BlockSpec((TM, 1024), row),
            pl.BlockSpec((TM, 512), row),
            pl.BlockSpec((TM, 256), row),
            pl.BlockSpec((TM, 128), row),
        ],
        out_shape=[
            jax.ShapeDtypeStruct((T, 1024), BF16),
            jax.ShapeDtypeStruct((T, 1024), BF16),
            jax.ShapeDtypeStruct((T, 512), BF16),
            jax.ShapeDtypeStruct((T, 256), F32),
            jax.ShapeDtypeStruct((T, 128), F32),
        ],
        compiler_params=_cparams(("arbitrary",), 40 * 1024 * 1024),
        name="mla_prep",
    )(proj, proj, proj, gcq, gckv, wuq, wuk_pad, wuv, gqn, gkn, *tables)


def mla_ctx_call(ckv_ctx, kr_pad, wuk_pad, wuv, gkn):
    n = ckv_ctx.shape[0]
    row = lambda i: (i, 0)
    const = lambda i: (0, 0)
    return pl.pallas_call(
        functools.partial(_mla_prep_kernel, with_q=False, norm_ckv=False, rope=False),
        grid=(n // TM,),
        in_specs=[
            pl.BlockSpec((TM, 256), row),
            pl.BlockSpec((TM, 128), row),
            pl.BlockSpec((256, 1024), const),
            pl.BlockSpec((256, 512), const),
            pl.BlockSpec((1, 128), const),
        ],
        out_specs=[pl.BlockSpec((TM, 1024), row), pl.BlockSpec((TM, 512), row)],
        out_shape=[jax.ShapeDtypeStruct((n, 1024), BF16), jax.ShapeDtypeStruct((n, 512), BF16)],
        compiler_params=_cparams(("arbitrary",), 40 * 1024 * 1024),
        name="mla_ctx",
    )(ckv_ctx, kr_pad, wuk_pad, wuv, gkn)


def _attn_dense_kernel(q_ref, k_ref, v_ref, *rest, scale, has_ctx):
    if has_ctx:
        kc_ref, vc_ref, o_ref = rest
    else:
        (o_ref,) = rest
    v = v_ref[...]
    outs = []
    for hh in range(2):
        sl = slice(hh * 128, (hh + 1) * 128)
        q = q_ref[:, sl]
        s = _dot_nt(q, k_ref[:, sl]) * scale
        m = jnp.max(s, axis=-1, keepdims=True)
        if has_ctx:
            sc = _dot_nt(q, kc_ref[:, sl]) * scale
            m = jnp.maximum(m, jnp.max(sc, axis=-1, keepdims=True))
        p = jnp.exp(s - m)
        l = jnp.sum(p, axis=-1, keepdims=True)
        o = _dot(p.astype(BF16), v)
        if has_ctx:
            pc = jnp.exp(sc - m)
            l = l + jnp.sum(pc, axis=-1, keepdims=True)
            o = o + _dot(pc.astype(BF16), vc_ref[...])
        outs.append(o / l)
    lane = lax.broadcasted_iota(I32, outs[0].shape, 1)
    o_ref[...] = jnp.where(lane < 64, outs[0], outs[1]).astype(o_ref.dtype)


def mla_attn_prompt_call(qb, kb, vb):
    blk = lambda b, hp: (b, hp)
    return pl.pallas_call(
        functools.partial(_attn_dense_kernel, scale=128 ** -0.5, has_ctx=False),
        grid=(N_PROMPT_SEQ, 4),
        in_specs=[
            pl.BlockSpec((PROMPT_LEN, 256), blk),
            pl.BlockSpec((PROMPT_LEN, 256), blk),
            pl.BlockSpec((PROMPT_LEN, 128), blk),
        ],
        out_specs=pl.BlockSpec((PROMPT_LEN, 128), blk),
        out_shape=jax.ShapeDtypeStruct((T_P, 512), BF16),
        compiler_params=_cparams(("arbitrary", "arbitrary")),
        name="mla_attn_prompt",
    )(qb, kb, vb)


def mla_attn_sample_call(qb, kb, vb, kc, vc):
    tq = 256
    nq = SAMPLE_LEN // tq
    qoff = T_P // tq
    soff = T_P // SAMPLE_LEN
    return pl.pallas_call(
        functools.partial(_attn_dense_kernel, scale=128 ** -0.5, has_ctx=True),
        grid=(N_SAMPLE_SEQ, 4, nq),
        in_specs=[
            pl.BlockSpec((tq, 256), lambda b, hp, qi: (qoff + b * nq + qi, hp)),
            pl.BlockSpec((SAMPLE_LEN, 256), lambda b, hp, qi: (soff + b, hp)),
            pl.BlockSpec((SAMPLE_LEN, 128), lambda b, hp, qi: (soff + b, hp)),
            pl.BlockSpec((PAST, 256), lambda b, hp, qi: (b, hp)),
            pl.BlockSpec((PAST, 128), lambda b, hp, qi: (b, hp)),
        ],
        out_specs=pl.BlockSpec((tq, 128), lambda b, hp, qi: (b * nq + qi, hp)),
        out_shape=jax.ShapeDtypeStruct((T_S, 512), BF16),
        compiler_params=_cparams(("arbitrary", "arbitrary", "arbitrary"), 48 * 1024 * 1024),
        name="mla_attn_sample",
    )(qb, kb, vb, kc, vc)


def _group_mean_sq(x, width):
    gi = lax.broadcasted_iota(I32, (width, width), 0) // 64
    gj = lax.broadcasted_iota(I32, (width, width), 1) // 64
    bd = jnp.where(gi == gj, 1.0, 0.0).astype(BF16)
    hi, lo = _split_bf16(x * x)
    return (_dot(hi, bd) + _dot(lo, bd)) * (1.0 / 64.0)


def _dup64(x, lane):
    r = pltpu.roll(x, 64, 1)
    return jnp.concatenate([jnp.where(lane < 64, x, r), jnp.where(lane >= 64, x, r)], axis=-1)


def _odd_prep_kernel(q_ref, kv_ref, gq_ref, gk_ref, c_ref, sa_ref, sb_ref, q_out, kd_out, vd_out, k_out, v_out):
    ct, sat, sbt = c_ref[...], sa_ref[...], sb_ref[...]
    qd = q_ref[...]
    qn = qd * lax.rsqrt(_group_mean_sq(qd, 512) + EPS)
    for p in range(4):
        sl = slice(p * 128, (p + 1) * 128)
        q_out[:, sl] = _rope(qn[:, sl] * gq_ref[...], ct, sat, sbt).astype(q_out.dtype)
    kd = kv_ref[:, 0:128]
    vd = kv_ref[:, 128:256]
    kn = kd * lax.rsqrt(_group_mean_sq(kd, 128) + EPS) * gk_ref[...]
    k_out[...] = kn
    v_out[...] = vd
    lane = lax.broadcasted_iota(I32, kd.shape, 1)
    kd_out[...] = _dup64(_rope(kn, ct, sat, sbt), lane).astype(kd_out.dtype)
    vd_out[...] = _dup64(vd, lane).astype(vd_out.dtype)


def odd_prep_call(proj, gq2, gk2, tables):
    row = lambda i: (i, 0)
    const = lambda i: (0, 0)
    return pl.pallas_call(
        _odd_prep_kernel,
        grid=(NBLK,),
        in_specs=[
            pl.BlockSpec((TM, 512), lambda i: (i, 1)),
            pl.BlockSpec((TM, 256), lambda i: (i, 4)),
            pl.BlockSpec((1, 128), const),
            pl.BlockSpec((1, 128), const),
            pl.BlockSpec((TM, 128), row),
            pl.BlockSpec((TM, 128), row),
            pl.BlockSpec((TM, 128), row),
        ],
        out_specs=[
            pl.BlockSpec((TM, 512), row),
            pl.BlockSpec((TM, 256), row),
            pl.BlockSpec((TM, 256), row),
            pl.BlockSpec((TM, 128), row),
            pl.BlockSpec((TM, 128), row),
        ],
        out_shape=[
            jax.ShapeDtypeStruct((T, 512), BF16),
            jax.ShapeDtypeStruct((T, 256), BF16),
            jax.ShapeDtypeStruct((T, 256), BF16),
            jax.ShapeDtypeStruct((T, 128), F32),
            jax.ShapeDtypeStruct((T, 128), F32),
        ],
        compiler_params=_cparams(("arbitrary",)),
        name="odd_prep",
    )(proj, proj, gq2, gk2, *tables)


def _sink_heads(q_ref, sink_ref, o_ref, score_fn, value_fn, scale):
    lane = lax.broadcasted_iota(I32, (q_ref.shape[0], 128), 1)
    for p in range(4):
        qp = q_ref[:, p * 128:(p + 1) * 128]
        halves = []
        for par in range(2):
            h = 2 * p + par
            kv = h // 4
            qm = jnp.where((lane >= 64) == (par == 1), qp, jnp.zeros_like(qp))
            ss = score_fn(qm, kv)
            sk = sink_ref[h]
            m = jnp.maximum(functools.reduce(jnp.maximum, [jnp.max(s, axis=-1, keepdims=True) for s in ss]), sk)
            ps = [jnp.exp(s - m) for s in ss]
            l = functools.reduce(jnp.add, [jnp.sum(pp, axis=-1, keepdims=True) for pp in ps]) + jnp.exp(sk - m)
            halves.append(value_fn(ps, kv) / l)
        o_ref[:, p * 128:(p + 1) * 128] = jnp.where(lane < 64, halves[0], halves[1]).astype(o_ref.dtype)


def _win_prompt_kernel(sink_ref, q_ref, k_ref, v_ref, o_ref, *, scale):
    def score_fn(qm, kv):
        return [_dot_nt(qm, k_ref[:, kv * 128:(kv + 1) * 128]) * scale]

    def value_fn(ps, kv):
        return _dot(ps[0].astype(BF16), v_ref[:, kv * 128:(kv + 1) * 128])

    _sink_heads(q_ref, sink_ref, o_ref, score_fn, value_fn, scale)


def win_prompt_call(sink, q, kd, vd):
    blk = lambda b: (b, 0)
    return pl.pallas_call(
        functools.partial(_win_prompt_kernel, scale=64 ** -0.5),
        grid=(N_PROMPT_SEQ,),
        in_specs=[
            pl.BlockSpec(memory_space=pltpu.SMEM),
            pl.BlockSpec((PROMPT_LEN, 512), blk),
            pl.BlockSpec((PROMPT_LEN, 256), blk),
            pl.BlockSpec((PROMPT_LEN, 256), blk),
        ],
        out_specs=pl.BlockSpec((PROMPT_LEN, 512), blk),
        out_shape=jax.ShapeDtypeStruct((T_P, 512), BF16),
        compiler_params=_cparams(("arbitrary",)),
        name="win_prompt",
    )(sink, q, kd, vd)


WIN = 128


def _win_sample_kernel(sink_ref, q_ref, k0, k1, k2, v0, v1, v2, kc_ref, vc_ref, o_ref, *, scale):
    qi = pl.program_id(1)
    start = qi * WIN
    r = lax.broadcasted_iota(I32, (WIN, 3 * WIN), 0)
    c = lax.broadcasted_iota(I32, (WIN, 3 * WIN), 1)
    kpos = start - WIN + c
    valid = (c - r >= 0) & (c - r <= 2 * WIN) & (kpos >= 0) & (kpos < SAMPLE_LEN)

    def score_fn(qm, kv):
        sl = slice(kv * 128, (kv + 1) * 128)
        kloc = jnp.concatenate([k0[:, sl], k1[:, sl], k2[:, sl]], axis=0)
        s_loc = jnp.where(valid, _dot_nt(qm, kloc) * scale, NEG_INF)
        return [s_loc, _dot_nt(qm, kc_ref[:, sl]) * scale]

    def value_fn(ps, kv):
        sl = slice(kv * 128, (kv + 1) * 128)
        vloc = jnp.concatenate([v0[:, sl], v1[:, sl], v2[:, sl]], axis=0)
        return _dot(ps[0].astype(BF16), vloc) + _dot(ps[1].astype(BF16), vc_ref[:, sl])

    _sink_heads(q_ref, sink_ref, o_ref, score_fn, value_fn, scale)


def win_sample_call(sink, q, kpad, vpad, kc, vc):
    nq = SAMPLE_LEN // WIN
    qoff = T_P // WIN
    per = nq + 2
    loc = lambda d: (lambda b, qi: (b * per + qi + d, 0))
    return pl.pallas_call(
        functools.partial(_win_sample_kernel, scale=64 ** -0.5),
        grid=(N_SAMPLE_SEQ, nq),
        in_specs=[
            pl.BlockSpec(memory_space=pltpu.SMEM),
            pl.BlockSpec((WIN, 512), lambda b, qi: (qoff + b * nq + qi, 0)),
            pl.BlockSpec((WIN, 256), loc(0)),
            pl.BlockSpec((WIN, 256), loc(1)),
            pl.BlockSpec((WIN, 256), loc(2)),
            pl.BlockSpec((WIN, 256), loc(0)),
            pl.BlockSpec((WIN, 256), loc(1)),
            pl.BlockSpec((WIN, 256), loc(2)),
            pl.BlockSpec((PAST, 256), lambda b, qi: (b, 0)),
            pl.BlockSpec((PAST, 256), lambda b, qi: (b, 0)),
        ],
        out_specs=pl.BlockSpec((WIN, 512), lambda b, qi: (b * nq + qi, 0)),
        out_shape=jax.ShapeDtypeStruct((T_S, 512), BF16),
        compiler_params=_cparams(("arbitrary", "arbitrary")),
        name="win_sample",
    )(sink, q, kpad, kpad, kpad, vpad, vpad, vpad, kc, vc)


def _pool_kernel(xp_ref, xc_ref, xn_ref, w_ref, sc_ref, o_ref):
    i = pl.program_id(0)
    jj = (i - NBLK_P) % SBLK
    first = jnp.logical_or(i < NBLK_P, jj == 0)
    last = jnp.logical_or(i < NBLK_P, jj == SBLK - 1)
    ext = TM + 2 * POOL_HALO
    r = lax.broadcasted_iota(I32, (TM, ext), 0)
    c = lax.broadcasted_iota(I32, (TM, ext), 1) - POOL_HALO
    ok = jnp.logical_and(jnp.logical_or(c >= 0, jnp.logical_not(first)), jnp.logical_or(c < TM, jnp.logical_not(last)))
    rr = lax.broadcasted_iota(I32, (TM, 1), 0)
    parts = []
    for g, w in enumerate(POOL_WINDOWS):
        sl = slice(g * 128, (g + 1) * 128)
        x = xc_ref[:, sl]
        xe = jnp.concatenate([xp_ref[:, sl], x, xn_ref[:, sl]], axis=0)
        lo = r - w // 2
        band = (c >= lo) & (c < lo + w) & ok
        a = jnp.where(band, 1.0, 0.0).astype(BF16)
        lo1 = rr - w // 2
        lo_c = jnp.where(first, jnp.maximum(lo1, 0), lo1)
        hi_c = jnp.where(last, jnp.minimum(lo1 + w, TM), lo1 + w)
        cnt = (hi_c - lo_c).astype(F32)
        hi_x, lo_x = _split_bf16(xe)
        pooled = (_dot(a, hi_x) + _dot(a, lo_x)) / cnt
        y = _dot((pooled - x).astype(BF16), w_ref[g])
        parts.append(y)
    o_ref[...] = (jnp.concatenate(parts, axis=-1) * sc_ref[...]).astype(o_ref.dtype)


def pool_call(proj, w_pool, scale):
    hb = TM // POOL_HALO
    nh = T // POOL_HALO
    return pl.pallas_call(
        _pool_kernel,
        grid=(NBLK,),
        in_specs=[
            pl.BlockSpec((POOL_HALO, 512), lambda i: (jnp.maximum(i * hb - 1, 0), 0)),
            pl.BlockSpec((TM, 512), lambda i: (i, 0)),
            pl.BlockSpec((POOL_HALO, 512), lambda i: (jnp.minimum((i + 1) * hb, nh - 1), 0)),
            pl.BlockSpec((4, 128, 128), lambda i: (0, 0, 0)),
            pl.BlockSpec((1, 512), lambda i: (0, 0)),
        ],
        out_specs=pl.BlockSpec((TM, 512), lambda i: (i, 0)),
        out_shape=jax.ShapeDtypeStruct((T, 512), BF16),
        compiler_params=_cparams(("arbitrary",)),
        name="pool",
    )(proj, proj, proj, w_pool, scale)


TOPK = 16
TK_TM = 128


def _top16(s, key, aux=None):
    cols = s.shape[1]
    r16 = lax.broadcasted_iota(I32, (TOPK, cols), 0)
    vals = jnp.zeros((TOPK, cols), F32)
    outs = jnp.zeros((TOPK, cols), F32)
    for r in range(TOPK):
        m = jnp.max(s, axis=0, keepdims=True)
        p = jnp.min(jnp.where(s == m, key, np.float32(1e9)), axis=0, keepdims=True)
        hit = key == p
        o = p if aux is None else jnp.max(jnp.where(hit, aux, -1.0), axis=0, keepdims=True)
        vals = jnp.where(r16 == r, m, vals)
        outs = jnp.where(r16 == r, o, outs)
        s = jnp.where(hit, -jnp.inf, s)
    return vals, outs


def _pair_candidates(v1, i1, v2, i2):
    cols = v1.shape[1]
    n8 = lax.broadcasted_iota(I32, (8, cols), 0)
    n16 = lax.broadcasted_iota(I32, (16, cols), 0)
    f8, f16 = n8.astype(F32), n16.astype(F32)
    sc, pos, ex = [], [], []

    def add(score, position, expert, mask):
        sc.append(score if mask is None else jnp.where(mask, score, -jnp.inf))
        pos.append(position)
        ex.append(expert)

    add(v1[0:1] + v2, f16, i1[0:1] * 128.0 + i2, None)
    add(v1[1:2] + v2[0:8], 16.0 + f8, i1[1:2] * 128.0 + i2[0:8], None)
    add(v1[2:3] + v2[0:8], 32.0 + f8, i1[2:3] * 128.0 + i2[0:8], n8 < 5)
    add(v1[3:4] + v2[0:8], 48.0 + f8, i1[3:4] * 128.0 + i2[0:8], n8 < 4)
    add(v1 + v2[0:1], f16 * 16.0, i1 * 128.0 + i2[0:1], n16 >= 4)
    add(v1[0:8] + v2[1:2], f8 * 16.0 + 1.0, i1[0:8] * 128.0 + i2[1:2], n8 >= 4)
    add(v1[0:8] + v2[2:3], f8 * 16.0 + 2.0, i1[0:8] * 128.0 + i2[2:3], n8 == 4)
    return jnp.concatenate(sc, axis=0), jnp.concatenate(pos, axis=0), jnp.concatenate(ex, axis=0)


def _peer_topk_kernel(q_ref, sk_ref, idx_ref, gate_ref):
    rowf = lax.broadcasted_iota(I32, (128, q_ref.shape[0]), 0).astype(F32)
    idx_rows, gate_rows = [], []
    for h in range(8):
        s1 = _dot_nt(sk_ref[2 * h], q_ref[:, (2 * h) * 128:(2 * h + 1) * 128])
        s2 = _dot_nt(sk_ref[2 * h + 1], q_ref[:, (2 * h + 1) * 128:(2 * h + 2) * 128])
        v1, i1 = _top16(s1, rowf)
        v2, i2 = _top16(s2, rowf)
        cand, pos, expert = _pair_candidates(v1, i1, v2, i2)
        tv, te = _top16(cand, pos, expert)
        e = jnp.exp(tv - tv[0:1])
        gate_rows.append(e / jnp.sum(e, axis=0, keepdims=True))
        idx_rows.append(te)
    idx_ref[...] = (jnp.concatenate(idx_rows, axis=0).T * 4.0).astype(I32)
    gate_ref[...] = jnp.concatenate(gate_rows, axis=0).T


def peer_topk_call(q, sk):
    row = lambda i: (i, 0)
    return pl.pallas_call(
        _peer_topk_kernel,
        grid=(T // TK_TM,),
        in_specs=[pl.BlockSpec((TK_TM, 2048), row), pl.BlockSpec((16, 128, 128), lambda i: (0, 0, 0))],
        out_specs=[pl.BlockSpec((TK_TM, NPICK), row), pl.BlockSpec((TK_TM, NPICK), row)],
        out_shape=[jax.ShapeDtypeStruct((T, NPICK), I32), jax.ShapeDtypeStruct((T, NPICK), F32)],
        compiler_params=_cparams(("arbitrary",)),
        name="peer_topk",
    )(q, sk)


def _gather_token(idx_smem, tab_ref, stage_ref, g, j):
    for k in range(NPICK):
        e4 = pl.multiple_of(idx_smem[g * PEER_TG + j, k], 4)
        stage_ref[pl.ds(j * 512 + 4 * k, 4), :] = tab_ref[pl.ds(e4, 4), :]


def _idx_copy(idx_hbm, idx_smem, sems, block, g):
    row0 = block * PEER_TB + g * PEER_TG
    return pltpu.make_async_copy(
        idx_hbm.at[pl.ds(row0, PEER_TG)], idx_smem.at[pl.ds(g * PEER_TG, PEER_TG)], sems.at[g])


def _gather_schedule(idx_hbm, idx_smem, sems, tab_ref, stage, begin, token, end):
    ngroup = PEER_TB // PEER_TG
    i = pl.program_id(0)
    last = pl.num_programs(0) - 1
    nxt = jnp.minimum(i + 1, last)
    nxt2 = jnp.minimum(i + 2, last)

    @pl.when(i == 0)
    def _():
        for g in range(ngroup):
            _idx_copy(idx_hbm, idx_smem, sems, 0, g).start()
        _idx_copy(idx_hbm, idx_smem, sems, 0, 0).wait()
        for j in range(PEER_TG):
            _gather_token(idx_smem, tab_ref, stage, 0, j)
        _idx_copy(idx_hbm, idx_smem, sems, nxt, 0).start()

    pending = None
    for g in range(ngroup):
        base = g * PEER_TG
        gn = (g + 1) % ngroup
        state = begin(base)
        _idx_copy(idx_hbm, idx_smem, sems, 0, gn).wait()
        for j in range(PEER_TG):
            state = token(state, stage, base, j)
            _gather_token(idx_smem, tab_ref, stage, gn, j)
            if j == 1 and pending is not None:
                end(*pending)
        _idx_copy(idx_hbm, idx_smem, sems, nxt if gn else nxt2, gn).start()
        pending = (state, base)
    end(*pending)

    @pl.when(i == last)
    def _():
        for g in range(ngroup):
            _idx_copy(idx_hbm, idx_smem, sems, 0, g).wait()


def _staged_tiles(stage_ref, j):
    return pltpu.bitcast(stage_ref[pl.ds(j * 512, 512), :], BF16)


def _peer_u_kernel(idx_hbm, x_ref, gate_ref, tab_ref, act_ref, idx_smem, stage, sems):
    col = lax.broadcasted_iota(I32, (16, 1024), 1)
    row = lax.broadcasted_iota(I32, (16, 1024), 0)
    diag = (col % 8) == (row % 8)
    gi = lax.broadcasted_iota(I32, (1024, 128), 0)
    gj = lax.broadcasted_iota(I32, (1024, 128), 1)
    gsum = jnp.where(gi // 8 == gj, 1.0, 0.0).astype(BF16)
    r8 = lax.broadcasted_iota(I32, (8, 1024), 0)

    def begin(base):
        return jnp.zeros((8, 1024), F32)

    def token(zs, stage_ref, base, j):
        xh, xl = _split_bf16(x_ref[base + j])
        lhs = jnp.concatenate([xh, xl], axis=0)
        z = _dot_nt(lhs, _staged_tiles(stage_ref, j))
        zr = jnp.sum(jnp.where(diag, z, 0.0), axis=0, keepdims=True)
        return jnp.where(r8 == j, zr, zs)

    def end(zs, base):
        zh, zl = _split_bf16(zs)
        sc = _dot(jnp.concatenate([zh, zl], axis=0), gsum)
        sc = sc[0:8] + sc[8:16]
        gate = gate_ref[pl.ds(base, PEER_TG), :]
        act = 0.5 * sc * (1.0 + lax.erf(sc * np.float32(1.0 / np.sqrt(2.0)))) * gate
        act_ref[pl.ds(base, PEER_TG), :] = act

    _gather_schedule(idx_hbm, idx_smem, sems, tab_ref.at[0], stage, begin, token, end)


def _peer_v_kernel(idx_hbm, act_ref, tab_ref, out_ref, idx_smem, stage, sems):
    ei = lax.broadcasted_iota(I32, (128, 1024), 0)
    ej = lax.broadcasted_iota(I32, (128, 1024), 1)
    expand = jnp.where(ej // 8 == ei, 1.0, 0.0).astype(BF16)
    col = lax.broadcasted_iota(I32, (8, 1024), 1)
    row = lax.broadcasted_iota(I32, (8, 1024), 0)
    diag = (col % 8) == row

    def begin(base):
        ah, al = _split_bf16(act_ref[pl.ds(base, PEER_TG), :])
        rep = _dot(jnp.concatenate([ah, al], axis=0), expand)
        return rep[0:8], rep[8:16]

    def token(rep, stage_ref, base, j):
        lh = jnp.where(diag, rep[0][j:j + 1, :], 0.0).astype(BF16)
        ll = jnp.where(diag, rep[1][j:j + 1, :], 0.0).astype(BF16)
        o = _dot(jnp.concatenate([lh, ll], axis=0), _staged_tiles(stage_ref, j))
        out_ref[base + j] = o[0:8] + o[8:16]
        return rep

    def end(rep, base):
        pass

    _gather_schedule(idx_hbm, idx_smem, sems, tab_ref.at[0], stage, begin, token, end)


def _gather_scratch():
    return [
        pltpu.SMEM((PEER_TB, NPICK), I32),
        pltpu.VMEM((PEER_TG * 512, 128), jnp.uint32),
        pltpu.SemaphoreType.DMA((PEER_TB // PEER_TG,)),
    ]


PACK_BE = 512


def _pack_kernel(t_ref, o_ref):
    x = t_ref[0]
    for s in range(4):
        lo = x[:, 256 * s:256 * s + 128].astype(BF16).astype(F32)
        hi = x[:, 256 * s + 128:256 * s + 256].astype(BF16).astype(F32)
        word = (pltpu.bitcast(lo, jnp.uint32) >> 16) | (pltpu.bitcast(hi, jnp.uint32) & jnp.uint32(0xFFFF0000))
        o_ref[0, pl.ds(s, PACK_BE, stride=4), :] = word


def pack_tables_call(tabs):
    depth = tabs.shape[0]
    return pl.pallas_call(
        _pack_kernel,
        grid=(depth, P_EXPERTS // PACK_BE),
        in_specs=[pl.BlockSpec((1, PACK_BE, D), lambda l, i: (l, i, 0))],
        out_specs=pl.BlockSpec((1, PACK_BE * 4, 128), lambda l, i: (l, i, 0)),
        out_shape=jax.ShapeDtypeStruct((depth, P_EXPERTS * 4, 128), jnp.uint32),
        compiler_params=_cparams(("arbitrary", "arbitrary")),
        name="pack_tables",
    )(tabs)


def peer_u_call(idx, x3, gate, tabs, layer):
    tb = PEER_TB
    return pl.pallas_call(
        _peer_u_kernel,
        grid=(T // tb,),
        in_specs=[
            pl.BlockSpec(memory_space=pl.ANY),
            pl.BlockSpec((tb, 8, 128), lambda i: (i, 0, 0)),
            pl.BlockSpec((tb, NPICK), lambda i: (i, 0)),
            pl.BlockSpec((1, P_EXPERTS * 4, 128), lambda i: (layer, 0, 0), pipeline_mode=pl.Buffered(1)),
        ],
        out_specs=pl.BlockSpec((tb, NPICK), lambda i: (i, 0)),
        out_shape=jax.ShapeDtypeStruct((T, NPICK), F32),
        scratch_shapes=_gather_scratch(),
        compiler_params=_cparams(("arbitrary",), VMEM_LIMIT),
        name="peer_u",
    )(idx, x3, gate, tabs)


def peer_v_call(idx, act, tabs, layer):
    tb = PEER_TB
    return pl.pallas_call(
        _peer_v_kernel,
        grid=(T // tb,),
        in_specs=[
            pl.BlockSpec(memory_space=pl.ANY),
            pl.BlockSpec((tb, NPICK), lambda i: (i, 0)),
            pl.BlockSpec((1, P_EXPERTS * 4, 128), lambda i: (layer, 0, 0), pipeline_mode=pl.Buffered(1)),
        ],
        out_specs=pl.BlockSpec((tb, 8, 128), lambda i: (i, 0, 0)),
        out_shape=jax.ShapeDtypeStruct((T, 8, 128), F32),
        scratch_shapes=_gather_scratch(),
        compiler_params=_cparams(("arbitrary",), VMEM_LIMIT),
        name="peer_v",
    )(idx, act, tabs)


def _even_in_weight(w):
    qa, ka, va, ra, glf, glb, cq, ckv, kr = jnp.split(w, np.cumsum([256, 256, 512, 512, 16, 16, 256, 256, 64])[:-1].tolist(), axis=-1)
    pad = jnp.zeros((D, 32), w.dtype)
    return jnp.concatenate([qa, ka, va, ra, cq, ckv, glf, glb, pad, kr], axis=-1).astype(BF16)


def _gate_up_pad(w_gu):
    z = jnp.zeros((2, 128, 256), F32)
    z = z.at[0, 0:16].set(w_gu[0])
    z = z.at[1, 16:32].set(w_gu[1])
    return z.astype(BF16)


def _uk_pad(w_uk):
    w = w_uk.reshape(256, 8, 64)
    return jnp.concatenate([w, jnp.zeros_like(w)], axis=-1).reshape(256, 1024).astype(BF16)


def _dup_heads(x):
    a, b = x[:, :64], x[:, 64:]
    return jnp.concatenate([a, a, b, b], axis=-1).astype(BF16)


def _pad_latent(x):
    x = x.reshape(N_SAMPLE_SEQ, SAMPLE_LEN, 256)
    x = jnp.pad(x, ((0, 0), (WIN, WIN), (0, 0)))
    return x.reshape(N_SAMPLE_SEQ * (SAMPLE_LEN + 2 * WIN), 256)


def kernel(x_prompt, x_sample, state_gla, cache_mla_ckv, cache_mla_krope, cache_win_kv, c, c_ctx, g_norm, w_ada, b_ada, w_in_even, w_gate_up, b_gate_up, g_gla_out, g_mla_cq, g_mla_ckv, w_mla_uq, w_mla_uk, w_mla_uv, g_mla_qn, g_mla_kn, w_out_even, w_in_odd, w_pool, pool_scale, g_win_qn, g_win_kn, win_sink, w_out_odd, peer_wq, peer_subkeys, peer_u, peer_v):
    depth = w_ada.shape[0]
    x = jnp.concatenate([x_prompt.reshape(T_P, D), x_sample.reshape(T_S, D)], axis=0)
    cond8 = jnp.concatenate([c_ctx[None], c, jnp.zeros((3, D), F32)], axis=0)
    mods_all = ada_call(cond8, w_ada, b_ada).reshape(depth, 8, 6, D)
    tables_b = _rope_tables("upper")
    tables_d = _rope_tables("all")
    tabs_u = pack_tables_call(peer_u)
    tabs_v = pack_tables_call(peer_v)

    gla_states, mla_ckv, mla_kr, win_kv = [], [], [], []
    y_prev, mods_prev = None, None
    for l in range(depth):
        mods = mods_all[l]
        w_in = _even_in_weight(w_in_even[l // 2]) if l % 2 == 0 else w_in_odd[l // 2].astype(BF16)
        outs = modproj_call(x, g_norm[l, 0], mods, w_in, 0, F32, False, y_prev, mods_prev)
        proj = outs[0]
        if y_prev is not None:
            x = outs[1]
        if l % 2 == 0:
            e = l // 2
            s0 = jnp.concatenate([jnp.zeros((N_PROMPT_SEQ, 2, 4, 64, 128), F32), state_gla[:, e]], axis=0)
            s0t = jnp.swapaxes(s0.reshape(-1, 2, 256, 128), -1, -2)
            o_f, o_b, sfin = gla_call(proj, _gate_up_pad(w_gate_up[e]), b_gate_up[e].reshape(2, 1, 256), s0t)
            sfin = jnp.swapaxes(sfin[:N_PROMPT_SEQ], -1, -2).reshape(N_PROMPT_SEQ, 2, 4, 64, 128)
            gla_states.append(sfin)
            wuk_pad = _uk_pad(w_mla_uk[e])
            wuv = w_mla_uv[e].astype(BF16)
            gkn = g_mla_kn[e].reshape(1, 128)
            qb, kb, vb, ckv_n, krf = mla_prep_call(
                proj, g_mla_cq[e].reshape(1, 256), g_mla_ckv[e].reshape(1, 256), w_mla_uq[e].astype(BF16),
                wuk_pad, wuv, g_mla_qn[e].reshape(1, 128), gkn, tables_b)
            mla_ckv.append(ckv_n[:T_P].reshape(N_PROMPT_SEQ, PROMPT_LEN, 256))
            mla_kr.append(krf[:T_P, 64:].reshape(N_PROMPT_SEQ, PROMPT_LEN, 64))
            kr_ctx = cache_mla_krope[:, e].reshape(N_SAMPLE_SEQ * PAST, 64)
            kr_pad = jnp.concatenate([jnp.zeros_like(kr_ctx), kr_ctx], axis=-1)
            kc, vc = mla_ctx_call(cache_mla_ckv[:, e].reshape(N_SAMPLE_SEQ * PAST, 256), kr_pad, wuk_pad, wuv, gkn)
            ob = jnp.concatenate([mla_attn_prompt_call(qb, kb, vb), mla_attn_sample_call(qb, kb, vb, kc, vc)], axis=0)
            x = outproj_call(x, o_f, o_b, proj, 2, g_gla_out[e].reshape(1, 128), ob, w_out_even[e].astype(BF16), mods, True)
        else:
            o = l // 2
            oc = pool_call(proj, w_pool[o].astype(BF16), pool_scale[o].reshape(1, 512))
            gq2 = jnp.tile(g_win_qn[o], 2).reshape(1, 128)
            gk2 = jnp.tile(g_win_kn[o], 2).reshape(1, 128)
            qw, kd, vd, kn, vn = odd_prep_call(proj, gq2, gk2, tables_d)
            k_p = kn[:T_P].reshape(N_PROMPT_SEQ, PROMPT_LEN, 2, 64)
            v_p = vn[:T_P].reshape(N_PROMPT_SEQ, PROMPT_LEN, 2, 64)
            win_kv.append(jnp.stack([k_p, v_p], axis=1))
            kc = _dup_heads(cache_win_kv[:, o, 0].reshape(N_SAMPLE_SEQ * PAST, 128))
            vc = _dup_heads(cache_win_kv[:, o, 1].reshape(N_SAMPLE_SEQ * PAST, 128))
            od = jnp.concatenate([
                win_prompt_call(win_sink[o], qw, kd, vd),
                win_sample_call(win_sink[o], qw, _pad_latent(kd[T_P:]), _pad_latent(vd[T_P:]), kc, vc)], axis=0)
            x = outproj_call(x, oc, oc, proj, 0, jnp.ones((1, 128), F32), od, w_out_odd[o].astype(BF16), mods, False)
        q, h = modproj_call(x, g_norm[l, 1], mods, peer_wq[l].astype(BF16), 3, BF16, True)
        idx, gate = peer_topk_call(q, peer_subkeys[l].reshape(16, 128, 128).astype(BF16))
        act = peer_u_call(idx, h.reshape(T, 8, 128), gate, tabs_u, l)
        y_prev, mods_prev = peer_v_call(idx, act, tabs_v, l).reshape(T, D), mods
    x = resid_call(x, y_prev, mods_prev, 5)

    return (
        x[:T_P].reshape(N_PROMPT_SEQ, PROMPT_LEN, D),
        x[T_P:].reshape(N_SAMPLE_SEQ, SAMPLE_LEN, D),
        jnp.stack(gla_states, axis=1),
        jnp.stack(mla_ckv, axis=1),
        jnp.stack(mla_kr, axis=1),
        jnp.stack(win_kv, axis=1),
    )
```
